```python
import jax, jax.numpy as jnp
from jax import lax
import numpy as np

D_MODEL = 1024
BATCH = 4
SEQ = 4096
DEPTH = 1
DEC_BATCH = 32
DEC_SEQ = 64
PAST_LEN = 2048

CHUNK = 64
D_CONV = 512
D_LRU = 512
D_MIX = D_CONV + D_LRU
N_LRU_HEADS = 8
LRU_HEAD_DIM = D_LRU // N_LRU_HEADS
CONV_A_WIDTH = 3
LRU_CONV_WIDTH = 4
LRU_C = 8.0
D_FF = 4 * D_MODEL
D_IN = 3 * D_CONV + 2 * D_LRU
EPS = 1e-6

kernel_name = "hymba_style_conv_rglru_stream_step"


def rms_norm(x, g):
    xf = x.astype(jnp.float32)
    y = xf * lax.rsqrt(jnp.mean(xf * xf, axis=-1, keepdims=True) + EPS)
    return (y * g.astype(jnp.float32)).astype(x.dtype)


def causal_depthwise_conv(x, buf, w):
    width = w.shape[0]
    t_len = x.shape[1]
    xp = jnp.concatenate([buf.astype(x.dtype), x], axis=1)
    y = xp[:, 0:t_len] * w[0]
    for k in range(1, width):
        y = y + xp[:, k:k + t_len] * w[k]
    return y, xp[:, t_len:]


def rg_lru(x, h0, wa, ba, wx, bx, a_param, reset_first):
    b_sz, t_len, _ = x.shape
    xh = x.reshape(b_sz, t_len, N_LRU_HEADS, LRU_HEAD_DIM)
    r = jax.nn.sigmoid(jnp.einsum("bthi,hij->bthj", xh, wa) + ba).reshape(b_sz, t_len, D_LRU)
    i = jax.nn.sigmoid(jnp.einsum("bthi,hij->bthj", xh, wx) + bx).reshape(b_sz, t_len, D_LRU)
    log_a = -LRU_C * r.astype(jnp.float32) * jax.nn.softplus(a_param.astype(jnp.float32))
    a = jnp.exp(log_a)
    mult = jnp.sqrt(-jnp.expm1(2.0 * log_a))
    if reset_first:
        mult = mult.at[:, 0].set(1.0)
    b = mult * (i * x).astype(jnp.float32)

    def step(h, ab):
        a_t, b_t = ab
        h = a_t * h + b_t
        return h, h

    h_last, hs = lax.scan(step, h0.astype(jnp.float32),
                          (jnp.swapaxes(a, 0, 1), jnp.swapaxes(b, 0, 1)))
    return jnp.swapaxes(hs, 0, 1).astype(x.dtype), h_last


def hybrid_layer(x, conv_a_buf, lru_buf, h0, reset_first, norm1_g, w_in, conv_a_w,
                 lru_conv_w, lru_conv_b, lru_wa, lru_ba, lru_wx, lru_bx, lru_a_param,
                 w_out, norm2_g, w_up, w_down):
    hn = rms_norm(x, norm1_g)
    proj = hn @ w_in
    g_b, g_c, x_a, x_l, g_l = jnp.split(
        proj, [D_CONV, 2 * D_CONV, 3 * D_CONV, 3 * D_CONV + D_LRU], axis=-1)
    c_a, new_conv_a = causal_depthwise_conv(g_c * x_a, conv_a_buf, conv_a_w)
    out_a = g_b * c_a
    c_l, new_lru_buf = causal_depthwise_conv(x_l, lru_buf, lru_conv_w)
    h_seq, h_last = rg_lru(c_l + lru_conv_b, h0, lru_wa, lru_ba, lru_wx, lru_bx,
                           lru_a_param, reset_first)
    out_b = h_seq * jax.nn.gelu(g_l)
    x = x + jnp.concatenate([out_a, out_b], axis=-1) @ w_out
    hm = rms_norm(x, norm2_g)
    x = x + jnp.square(jax.nn.relu(hm @ w_up)) @ w_down
    return x, new_conv_a, new_lru_buf, h_last


def setup_inputs(seed: int = 0) -> dict:
    key = jax.random.key(seed)
    ks = jax.random.split(key, 24)
    f32 = jnp.float32
    nrm = lambda k, s, scale: (jax.random.normal(k, s, f32) * scale)
    r0 = jax.random.uniform(ks[13], (DEPTH, D_LRU), f32, 0.9, 0.999)
    return {
        "x_prompt": nrm(ks[0], (BATCH, SEQ, D_MODEL), 1.0),
        "x_sample": nrm(ks[1], (DEC_BATCH, DEC_SEQ, D_MODEL), 1.0),
        "state_conv_a": nrm(ks[2], (DEPTH, DEC_BATCH, CONV_A_WIDTH - 1, D_CONV), 1.0),
        "state_lru_conv": nrm(ks[3], (DEPTH, DEC_BATCH, LRU_CONV_WIDTH - 1, D_LRU), 1.0),
        "state_lru_h": nrm(ks[4], (DEPTH, DEC_BATCH, D_LRU), 0.5),
        "norm1_g": 1.0 + nrm(ks[5], (DEPTH, D_MODEL), 0.02),
        "w_in": nrm(ks[6], (DEPTH, D_MODEL, D_IN), D_MODEL ** -0.5),
        "conv_a_w": nrm(ks[7], (DEPTH, CONV_A_WIDTH, D_CONV), CONV_A_WIDTH ** -0.5),
        "lru_conv_w": nrm(ks[8], (DEPTH, LRU_CONV_WIDTH, D_LRU), LRU_CONV_WIDTH ** -0.5),
        "lru_conv_b": nrm(ks[9], (DEPTH, D_LRU), 0.01),
        "lru_wa": nrm(ks[10], (DEPTH, N_LRU_HEADS, LRU_HEAD_DIM, LRU_HEAD_DIM), LRU_HEAD_DIM ** -0.5),
        "lru_ba": nrm(ks[11], (DEPTH, N_LRU_HEADS, LRU_HEAD_DIM), 0.01),
        "lru_wx": nrm(ks[12], (DEPTH, N_LRU_HEADS, LRU_HEAD_DIM, LRU_HEAD_DIM), LRU_HEAD_DIM ** -0.5),
        "lru_bx": nrm(ks[14], (DEPTH, N_LRU_HEADS, LRU_HEAD_DIM), 0.01),
        "lru_a_param": jnp.log(jnp.expm1(-jnp.log(r0))),
        "w_out": nrm(ks[15], (DEPTH, D_MIX, D_MODEL), D_MIX ** -0.5),
        "norm2_g": 1.0 + nrm(ks[16], (DEPTH, D_MODEL), 0.02),
        "w_up": nrm(ks[17], (DEPTH, D_MODEL, D_FF), D_MODEL ** -0.5),
        "w_down": nrm(ks[18], (DEPTH, D_FF, D_MODEL), D_FF ** -0.5),
        "norm_f_g": 1.0 + nrm(ks[19], (D_MODEL,), 0.02),
    }


def reference(x_prompt, x_sample, state_conv_a, state_lru_conv, state_lru_h,
              norm1_g, w_in, conv_a_w, lru_conv_w, lru_conv_b, lru_wa, lru_ba,
              lru_wx, lru_bx, lru_a_param, w_out, norm2_g, w_up, w_down, norm_f_g):
    n_p = x_prompt.shape[0]
    xp = x_prompt
    xs = x_sample
    pa, pl, ph, sa, sl, sh = [], [], [], [], [], []
    for l in range(DEPTH):
        layer_w = (norm1_g[l], w_in[l], conv_a_w[l], lru_conv_w[l], lru_conv_b[l],
                   lru_wa[l], lru_ba[l], lru_wx[l], lru_bx[l], lru_a_param[l],
                   w_out[l], norm2_g[l], w_up[l], w_down[l])
        zero_a = jnp.zeros((n_p, CONV_A_WIDTH - 1, D_CONV), xp.dtype)
        zero_l = jnp.zeros((n_p, LRU_CONV_WIDTH - 1, D_LRU), xp.dtype)
        zero_h = jnp.zeros((n_p, D_LRU), jnp.float32)
        xp, ca, cl, hl = hybrid_layer(xp, zero_a, zero_l, zero_h, True, *layer_w)
        pa.append(ca); pl.append(cl); ph.append(hl)
        xs, ca, cl, hl = hybrid_layer(xs, state_conv_a[l], state_lru_conv[l],
                                      state_lru_h[l], False, *layer_w)
        sa.append(ca); sl.append(cl); sh.append(hl)
    y_prompt = rms_norm(xp, norm_f_g)
    y_sample = rms_norm(xs, norm_f_g)
    return (y_prompt, y_sample, jnp.stack(pa), jnp.stack(pl), jnp.stack(ph),
            jnp.stack(sa), jnp.stack(sl), jnp.stack(sh))
```

```python
import functools

import jax
import jax.numpy as jnp
from jax import lax
from jax.experimental import pallas as pl
from jax.experimental.pallas import tpu as pltpu

LRU_C = 8.0
EPS = 1e-6
GELU_K0 = 0.7978845608028654
GELU_K1 = 0.044715

V7X_SUBLANES = 8
V7X_MXU_DIM = 256
V7X_VMEM_LIMIT_BYTES = 58 * 1024 * 1024
HIST_ROWS = V7X_SUBLANES
PROMPT_CHUNK = 256
SCAN_UNROLL = 8


def _rms_norm(x, g):
    return x * lax.rsqrt(jnp.mean(x * x, axis=-1, keepdims=True) + EPS) * g


def _sigmoid(x):
    return 0.5 * jnp.tanh(0.5 * x) + 0.5


def _gelu_tanh(x):
    return 0.5 * x * (1.0 + jnp.tanh(GELU_K0 * (x + GELU_K1 * (x * x * x))))


def _softplus(x):
    return jnp.maximum(x, 0.0) + jnp.log1p(jnp.exp(-jnp.abs(x)))


def _layer_kernel(
    x_ref, sa_ref, sl_ref, sh_ref,
    g1_ref, w_in_ref, caw_ref, lcw_ref, lcb_ref, wg_ref, ba_ref, bx_ref, ap_ref,
    w_out_ref, g2_ref, w_up_ref, w_down_ref, gf_ref,
    y_ref, na_ref, nl_ref, nh_ref,
    ext_a, ext_l, h_c, a_s, b_s, h_s,
    *, reset_first):
    n_seq, seg, d_model = x_ref.shape
    rows = n_seq * seg
    d_conv = caw_ref.shape[1]
    d_lru = lcw_ref.shape[1]
    wa = caw_ref.shape[0]
    wl = lcw_ref.shape[0]
    chunk = pl.program_id(1)

    @pl.when(chunk == 0)
    def _():
        ext_a[:, HIST_ROWS - (wa - 1):HIST_ROWS, :] = sa_ref[...]
        ext_l[:, HIST_ROWS - (wl - 1):HIST_ROWS, :] = sl_ref[...]
        h_c[...] = sh_ref[...]

    x = x_ref[...].reshape(rows, d_model)
    hn = _rms_norm(x, g1_ref[...]).astype(jnp.bfloat16)
    proj = jnp.dot(hn, w_in_ref[...], preferred_element_type=jnp.float32)
    g_b = proj[:, 0:d_conv]
    g_c = proj[:, d_conv:2 * d_conv]
    x_a = proj[:, 2 * d_conv:3 * d_conv]
    x_l = proj[:, 3 * d_conv:3 * d_conv + d_lru]
    g_l = proj[:, 3 * d_conv + d_lru:]

    ext_a[:, HIST_ROWS:, :] = (g_c * x_a).reshape(n_seq, seg, d_conv)
    c_a = None
    for k in range(wa):
        lo = HIST_ROWS - (wa - 1) + k
        term = ext_a[:, lo:lo + seg, :].reshape(rows, d_conv) * caw_ref[k:k + 1, :]
        c_a = term if c_a is None else c_a + term
    out_a = g_b * c_a
    new_a = ext_a[:, seg + HIST_ROWS - (wa - 1):seg + HIST_ROWS, :]
    na_ref[...] = new_a
    ext_a[:, HIST_ROWS - (wa - 1):HIST_ROWS, :] = new_a

    ext_l[:, HIST_ROWS:, :] = x_l.reshape(n_seq, seg, d_lru)
    c_l = None
    for k in range(wl):
        lo = HIST_ROWS - (wl - 1) + k
        term = ext_l[:, lo:lo + seg, :].reshape(rows, d_lru) * lcw_ref[k:k + 1, :]
        c_l = term if c_l is None else c_l + term
    new_l = ext_l[:, seg + HIST_ROWS - (wl - 1):seg + HIST_ROWS, :]
    nl_ref[...] = new_l
    ext_l[:, HIST_ROWS - (wl - 1):HIST_ROWS, :] = new_l

    c = c_l + lcb_ref[...]
    c_bf = c.astype(jnp.bfloat16)
    neg_rate = -LRU_C * _softplus(ap_ref[...])
    for p in range(d_lru // V7X_MXU_DIM):
        cols = slice(p * V7X_MXU_DIM, (p + 1) * V7X_MXU_DIM)
        gates = jnp.dot(c_bf[:, cols], wg_ref[p], preferred_element_type=jnp.float32)
        r = _sigmoid(gates[:, :V7X_MXU_DIM] + ba_ref[:, cols])
        i = _sigmoid(gates[:, V7X_MXU_DIM:] + bx_ref[:, cols])
        a = jnp.exp(r * neg_rate[:, cols])
        mult = jnp.sqrt(1.0 - a * a)
        ic = i * c[:, cols]
        a_s[:, cols] = a
        b_s[:, cols] = mult * ic
        if reset_first:
            @pl.when(chunk == 0)
            def _():
                for s in range(n_seq):
                    b_s[s * seg:s * seg + 1, cols] = ic[s * seg:s * seg + 1, :]

    for s in range(n_seq):
        def step(t, h, s=s):
            row = s * seg + t
            h = a_s[pl.ds(row, 1), :] * h + b_s[pl.ds(row, 1), :]
            h_s[pl.ds(row, 1), :] = h
            return h
        h_c[s] = lax.fori_loop(0, seg, step, h_c[s], unroll=SCAN_UNROLL)
    nh_ref[...] = h_c[...]

    out_b = h_s[...] * _gelu_tanh(g_l)
    mix = jnp.concatenate([out_a, out_b], axis=-1).astype(jnp.bfloat16)
    x1 = x + jnp.dot(mix, w_out_ref[...], preferred_element_type=jnp.float32)

    hm = _rms_norm(x1, g2_ref[...]).astype(jnp.bfloat16)
    up = jnp.dot(hm, w_up_ref[...], preferred_element_type=jnp.float32)
    act = jnp.square(jnp.maximum(up, 0.0)).astype(jnp.bfloat16)
    x2 = x1 + jnp.dot(act, w_down_ref[...], preferred_element_type=jnp.float32)
    y_ref[...] = _rms_norm(x2, gf_ref[...]).reshape(n_seq, seg, d_model)


def _resident(shape):
    zeros = (0,) * len(shape)
    return pl.BlockSpec(shape, lambda i, j: zeros, pipeline_mode=pl.Buffered(1))


def _run_layer(x, st_a, st_l, st_h, params, *, seq_per_block, chunk, reset_first, name):
    n_seq, t_len, d_model = x.shape
    d_conv = st_a.shape[-1]
    d_lru = st_l.shape[-1]
    assert n_seq % seq_per_block == 0 and t_len % chunk == 0
    assert chunk % V7X_SUBLANES == 0
    rows = seq_per_block * chunk
    grid = (n_seq // seq_per_block, t_len // chunk)

    def per_seq(a):
        return pl.BlockSpec((seq_per_block,) + a.shape[1:], lambda i, j: (i, 0, 0))

    x_spec = pl.BlockSpec((seq_per_block, chunk, d_model), lambda i, j: (i, j, 0))
    in_specs = [x_spec, per_seq(st_a), per_seq(st_l), per_seq(st_h)]
    in_specs += [_resident(p.shape) for p in params]
    out_shape = (
        jax.ShapeDtypeStruct(x.shape, jnp.float32),
        jax.ShapeDtypeStruct(st_a.shape, jnp.float32),
        jax.ShapeDtypeStruct(st_l.shape, jnp.float32),
        jax.ShapeDtypeStruct(st_h.shape, jnp.float32),
    )
    out_specs = (x_spec, per_seq(st_a), per_seq(st_l), per_seq(st_h))
    scratch = [
        pltpu.VMEM((seq_per_block, HIST_ROWS + chunk, d_conv), jnp.float32),
        pltpu.VMEM((seq_per_block, HIST_ROWS + chunk, d_lru), jnp.float32),
        pltpu.VMEM((seq_per_block, 1, d_lru), jnp.float32),
        pltpu.VMEM((rows, d_lru), jnp.float32),
        pltpu.VMEM((rows, d_lru), jnp.float32),
        pltpu.VMEM((rows, d_lru), jnp.float32),
    ]
    return pl.pallas_call(
        functools.partial(_layer_kernel, reset_first=reset_first),
        grid=grid,
        in_specs=in_specs,
        out_specs=out_specs,
        out_shape=out_shape,
        scratch_shapes=scratch,
        compiler_params=pltpu.CompilerParams(
            dimension_semantics=("arbitrary", "arbitrary"),
            vmem_limit_bytes=V7X_VMEM_LIMIT_BYTES),
        name=name,
    )(x, st_a, st_l, st_h, *params)


def _block_diag(w):
    n_heads, d, _ = w.shape
    eye = jnp.eye(n_heads, dtype=w.dtype)
    return (w[:, :, None, :] * eye[:, None, :, None]).reshape(n_heads * d, n_heads * d)


def _gate_weights(wa, wx):
    bd_a, bd_x = _block_diag(wa), _block_diag(wx)
    d_lru = bd_a.shape[0]
    groups = []
    for p in range(d_lru // V7X_MXU_DIM):
        sl = slice(p * V7X_MXU_DIM, (p + 1) * V7X_MXU_DIM)
        groups.append(jnp.concatenate([bd_a[sl, sl], bd_x[sl, sl]], axis=1))
    return jnp.stack(groups).astype(jnp.bfloat16)


def kernel(x_prompt, x_sample, state_conv_a, state_lru_conv, state_lru_h, norm1_g, w_in, conv_a_w, lru_conv_w, lru_conv_b, lru_wa, lru_ba, lru_wx, lru_bx, lru_a_param, w_out, norm2_g, w_up, w_down, norm_f_g):
    depth = w_in.shape[0]
    assert depth == 1
    n_p = x_prompt.shape[0]
    n_s, dec_seq, _ = x_sample.shape
    d_conv = conv_a_w.shape[-1]
    d_lru = lru_conv_w.shape[-1]
    bf16 = jnp.bfloat16
    l = 0
    params = (
        norm1_g[l][None, :], w_in[l].astype(bf16), conv_a_w[l], lru_conv_w[l],
        lru_conv_b[l][None, :], _gate_weights(lru_wa[l], lru_wx[l]),
        lru_ba[l].reshape(1, d_lru), lru_bx[l].reshape(1, d_lru),
        lru_a_param[l][None, :], w_out[l].astype(bf16), norm2_g[l][None, :],
        w_up[l].astype(bf16), w_down[l].astype(bf16), norm_f_g[None, :],
    )
    f32 = jnp.float32
    yp, pa, pl_, ph = _run_layer(
        x_prompt,
        jnp.zeros((n_p, conv_a_w.shape[1] - 1, d_conv), f32),
        jnp.zeros((n_p, lru_conv_w.shape[1] - 1, d_lru), f32),
        jnp.zeros((n_p, 1, d_lru), f32),
        params, seq_per_block=1, chunk=PROMPT_CHUNK, reset_first=True,
        name="layer_prompt")
    ys, sa, sl, sh = _run_layer(
        x_sample, state_conv_a[l], state_lru_conv[l], state_lru_h[l][:, None, :],
        params, seq_per_block=PROMPT_CHUNK // dec_seq, chunk=dec_seq,
        reset_first=False, name="layer_sample")
    return (yp, ys, pa[None], pl_[None], ph[:, 0, :][None],
            sa[None], sl[None], sh[:, 0, :][None])
```

```python
import functools

import jax
import jax.numpy as jnp
from jax import lax
from jax.experimental import pallas as pl
from jax.experimental.pallas import tpu as pltpu

LRU_C = 8.0
EPS = 1e-6
GELU_K0 = 0.7978845608028654
GELU_K1 = 0.044715

V7X_SUBLANES = 8
V7X_MXU_DIM = 256
V7X_VMEM_LIMIT_BYTES = 58 * 1024 * 1024
HIST_ROWS = V7X_SUBLANES
PROMPT_TILE_ROWS = 512
SAMPLE_TILE_ROWS = 256
FF_CHUNK = 1024


def _rms_norm(x, g):
    return x * lax.rsqrt(jnp.mean(x * x, axis=-1, keepdims=True) + EPS) * g


def _sigmoid(x):
    return 0.5 * jnp.tanh(0.5 * x) + 0.5


def _gelu_tanh(x):
    return 0.5 * x * (1.0 + jnp.tanh(GELU_K0 * (x + GELU_K1 * (x * x * x))))


def _softplus(x):
    return jnp.maximum(x, 0.0) + jnp.log1p(jnp.exp(-jnp.abs(x)))


def _causal_conv(ext, w_ref, n_seq, seg):
    width, chans = w_ref.shape
    acc = None
    for k in range(width):
        lo = HIST_ROWS - (width - 1) + k
        term = ext[:, lo:lo + seg, :].reshape(n_seq * seg, chans) * w_ref[k:k + 1, :]
        acc = term if acc is None else acc + term
    return acc


def _layer_kernel(
    x_ref, sa_ref, sl_ref, sh_ref,
    g1_ref, w_in_ref, caw_ref, lcw_ref, lcb_ref, wg_ref, ba_ref, bx_ref, ap_ref,
    w_out_ref, g2_ref, w_up_ref, w_down_ref, gf_ref,
    y_ref, na_ref, nl_ref, nh_ref,
    ext_a, ext_l, h_c, a_s, b_s, h_s, x_s, oa_s, gl_s,
    *, n_chunks, reset_first):
    n_seq, seg, d_model = x_ref.shape
    rows = n_seq * seg
    d_conv = caw_ref.shape[1]
    d_lru = lcw_ref.shape[1]
    wa = caw_ref.shape[0]
    wl = lcw_ref.shape[0]
    g = pl.program_id(0)
    front_chunk = g % n_chunks
    back_chunk = (g + n_chunks - 1) % n_chunks

    wr = g % 2
    rd = 1 - wr
    a_rd, b_rd, x_rd, oa_rd, gl_rd = (r.at[rd] for r in (a_s, b_s, x_s, oa_s, gl_s))
    a_wr, b_wr, x_wr, oa_wr, gl_wr = (r.at[wr] for r in (a_s, b_s, x_s, oa_s, gl_s))

    @pl.when(g == 0)
    def _():
        for ref in (h_c, a_s, b_s, x_s, oa_s, gl_s):
            ref[...] = jnp.zeros(ref.shape, ref.dtype)

    @pl.when(front_chunk == 0)
    def _():
        ext_a[:, HIST_ROWS - (wa - 1):HIST_ROWS, :] = sa_ref[...]
        ext_l[:, HIST_ROWS - (wl - 1):HIST_ROWS, :] = sl_ref[...]

    @pl.when(back_chunk == 0)
    def _():
        h_c[...] = sh_ref[...]


    x = x_ref[...].reshape(rows, d_model)
    hn = _rms_norm(x, g1_ref[...]).astype(jnp.bfloat16)
    proj = jnp.dot(hn, w_in_ref[...], preferred_element_type=jnp.float32)
    g_b = proj[:, 0:d_conv]
    g_c = proj[:, d_conv:2 * d_conv]
    x_a = proj[:, 2 * d_conv:3 * d_conv]
    x_l = proj[:, 3 * d_conv:3 * d_conv + d_lru]
    g_l = proj[:, 3 * d_conv + d_lru:]

    for s in range(n_seq):
        h = h_c[s]
        for t in range(seg):
            r = s * seg + t
            h = a_rd[r:r + 1, :] * h + b_rd[r:r + 1, :]
            h_s[r:r + 1, :] = h
        h_c[s] = h
    nh_ref[...] = h_c[...]

    out_b = h_s[...] * gl_rd[...]
    mix = jnp.concatenate([oa_rd[...], out_b.astype(jnp.bfloat16)], axis=-1)
    x1 = x_rd[...] + jnp.dot(mix, w_out_ref[...], preferred_element_type=jnp.float32)
    hm = _rms_norm(x1, g2_ref[...]).astype(jnp.bfloat16)
    d_ff = w_up_ref.shape[1]
    n_ff = d_ff // FF_CHUNK

    def mlp_chunk(f, acc):
        cols = slice(f * FF_CHUNK, (f + 1) * FF_CHUNK)
        up = jnp.dot(hm, w_up_ref[:, cols], preferred_element_type=jnp.float32)
        act = jnp.square(jnp.maximum(up, 0.0)).astype(jnp.bfloat16)
        return acc + jnp.dot(act, w_down_ref[cols, :], preferred_element_type=jnp.float32)

    x2 = x1
    for f in range(n_ff // 2):
        x2 = mlp_chunk(f, x2)

    ext_a[:, HIST_ROWS:, :] = (g_c * x_a).reshape(n_seq, seg, d_conv)
    out_a = g_b * _causal_conv(ext_a, caw_ref, n_seq, seg)
    new_a = ext_a[:, seg + HIST_ROWS - (wa - 1):seg + HIST_ROWS, :]
    na_ref[...] = new_a
    ext_a[:, HIST_ROWS - (wa - 1):HIST_ROWS, :] = new_a

    ext_l[:, HIST_ROWS:, :] = x_l.reshape(n_seq, seg, d_lru)
    c = _causal_conv(ext_l, lcw_ref, n_seq, seg) + lcb_ref[...]
    new_l = ext_l[:, seg + HIST_ROWS - (wl - 1):seg + HIST_ROWS, :]
    nl_ref[...] = new_l
    ext_l[:, HIST_ROWS - (wl - 1):HIST_ROWS, :] = new_l

    c_bf = c.astype(jnp.bfloat16)
    neg_rate = -LRU_C * _softplus(ap_ref[...])
    for p in range(d_lru // V7X_MXU_DIM):
        cols = slice(p * V7X_MXU_DIM, (p + 1) * V7X_MXU_DIM)
        gates = jnp.dot(c_bf[:, cols], wg_ref[p], preferred_element_type=jnp.float32)
        r_gate = _sigmoid(gates[:, :V7X_MXU_DIM] + ba_ref[:, cols])
        i_gate = _sigmoid(gates[:, V7X_MXU_DIM:] + bx_ref[:, cols])
        a = jnp.exp(r_gate * neg_rate[:, cols])
        mult = jnp.sqrt(1.0 - a * a)
        ic = i_gate * c[:, cols]
        a_wr[:, cols] = a
        b_wr[:, cols] = mult * ic
        if reset_first:
            first = front_chunk == 0
            for s in range(n_seq):
                r0 = s * seg
                b_wr[r0:r0 + 1, cols] = jnp.where(
                    first, ic[r0:r0 + 1, :], (mult * ic)[r0:r0 + 1, :])

    x_wr[...] = x
    oa_wr[...] = out_a.astype(jnp.bfloat16)
    gl_wr[...] = _gelu_tanh(g_l)

    for f in range(n_ff // 2, n_ff):
        x2 = mlp_chunk(f, x2)
    y_ref[...] = _rms_norm(x2, gf_ref[...]).reshape(n_seq, seg, d_model)


def _resident(shape):
    zeros = (0,) * len(shape)
    return pl.BlockSpec(shape, lambda g: zeros, pipeline_mode=pl.Buffered(1))


def _run_layer(x, st_a, st_l, st_h, params, *, seq_per_block, chunk, reset_first, name):
    n_seq, t_len, d_model = x.shape
    d_conv = st_a.shape[-1]
    d_lru = st_l.shape[-1]
    assert n_seq % seq_per_block == 0 and t_len % chunk == 0
    assert chunk % V7X_SUBLANES == 0
    rows = seq_per_block * chunk
    n_chunks = t_len // chunk
    n_tiles = (n_seq // seq_per_block) * n_chunks

    def front_tile(g):
        return jnp.minimum(g, n_tiles - 1)

    def back_tile(g):
        return jnp.maximum(g - 1, 0)

    def tile_spec(tile_of):
        return pl.BlockSpec(
            (seq_per_block, chunk, d_model),
            lambda g: (tile_of(g) // n_chunks, tile_of(g) % n_chunks, 0))

    def per_seq(a, tile_of):
        return pl.BlockSpec((seq_per_block,) + a.shape[1:],
                            lambda g: (tile_of(g) // n_chunks, 0, 0))

    in_specs = [tile_spec(front_tile), per_seq(st_a, front_tile),
                per_seq(st_l, front_tile), per_seq(st_h, back_tile)]
    in_specs += [_resident(p.shape) for p in params]
    out_shape = (
        jax.ShapeDtypeStruct(x.shape, jnp.float32),
        jax.ShapeDtypeStruct(st_a.shape, jnp.float32),
        jax.ShapeDtypeStruct(st_l.shape, jnp.float32),
        jax.ShapeDtypeStruct(st_h.shape, jnp.float32),
    )
    out_specs = (tile_spec(back_tile), per_seq(st_a, front_tile),
                 per_seq(st_l, front_tile), per_seq(st_h, back_tile))
    f32 = jnp.float32
    scratch = [
        pltpu.VMEM((seq_per_block, HIST_ROWS + chunk, d_conv), f32),
        pltpu.VMEM((seq_per_block, HIST_ROWS + chunk, d_lru), f32),
        pltpu.VMEM((seq_per_block, 1, d_lru), f32),
        pltpu.VMEM((2, rows, d_lru), f32),
        pltpu.VMEM((2, rows, d_lru), f32),
        pltpu.VMEM((rows, d_lru), f32),
        pltpu.VMEM((2, rows, d_model), f32),
        pltpu.VMEM((2, rows, d_conv), jnp.bfloat16),
        pltpu.VMEM((2, rows, d_lru), f32),
    ]
    return pl.pallas_call(
        functools.partial(_layer_kernel, n_chunks=n_chunks, reset_first=reset_first),
        grid=(n_tiles + 1,),
        in_specs=in_specs,
        out_specs=out_specs,
        out_shape=out_shape,
        scratch_shapes=scratch,
        compiler_params=pltpu.CompilerParams(
            dimension_semantics=("arbitrary",),
            vmem_limit_bytes=V7X_VMEM_LIMIT_BYTES),
        name=name,
    )(x, st_a, st_l, st_h, *params)


def _block_diag(w):
    n_heads, d, _ = w.shape
    eye = jnp.eye(n_heads, dtype=w.dtype)
    return (w[:, :, None, :] * eye[:, None, :, None]).reshape(n_heads * d, n_heads * d)


def _gate_weights(wa, wx):
    bd_a, bd_x = _block_diag(wa), _block_diag(wx)
    d_lru = bd_a.shape[0]
    groups = []
    for p in range(d_lru // V7X_MXU_DIM):
        sl = slice(p * V7X_MXU_DIM, (p + 1) * V7X_MXU_DIM)
        groups.append(jnp.concatenate([bd_a[sl, sl], bd_x[sl, sl]], axis=1))
    return jnp.stack(groups).astype(jnp.bfloat16)


def kernel(x_prompt, x_sample, state_conv_a, state_lru_conv, state_lru_h, norm1_g, w_in, conv_a_w, lru_conv_w, lru_conv_b, lru_wa, lru_ba, lru_wx, lru_bx, lru_a_param, w_out, norm2_g, w_up, w_down, norm_f_g):
    depth = w_in.shape[0]
    assert depth == 1
    n_p = x_prompt.shape[0]
    n_s, dec_seq, _ = x_sample.shape
    d_conv = conv_a_w.shape[-1]
    d_lru = lru_conv_w.shape[-1]
    bf16 = jnp.bfloat16
    l = 0
    params = (
        norm1_g[l][None, :], w_in[l].astype(bf16), conv_a_w[l], lru_conv_w[l],
        lru_conv_b[l][None, :], _gate_weights(lru_wa[l], lru_wx[l]),
        lru_ba[l].reshape(1, d_lru), lru_bx[l].reshape(1, d_lru),
        lru_a_param[l][None, :], w_out[l].astype(bf16), norm2_g[l][None, :],
        w_up[l].astype(bf16), w_down[l].astype(bf16), norm_f_g[None, :],
    )
    f32 = jnp.float32
    yp, pa, pl_, ph = _run_layer(
        x_prompt,
        jnp.zeros((n_p, conv_a_w.shape[1] - 1, d_conv), f32),
        jnp.zeros((n_p, lru_conv_w.shape[1] - 1, d_lru), f32),
        jnp.zeros((n_p, 1, d_lru), f32),
        params, seq_per_block=1, chunk=PROMPT_TILE_ROWS, reset_first=True,
        name="layer_prompt")
    ys, sa, sl, sh = _run_layer(
        x_sample, state_conv_a[l], state_lru_conv[l], state_lru_h[l][:, None, :],
        params, seq_per_block=SAMPLE_TILE_ROWS // dec_seq, chunk=dec_seq,
        reset_first=False, name="layer_sample")
    return (yp, ys, pa[None], pl_[None], ph[:, 0, :][None],
            sa[None], sl[None], sh[:, 0, :][None])
```

```python
import functools

import jax
import jax.numpy as jnp
from jax import lax
from jax.experimental import pallas as pl
from jax.experimental.pallas import tpu as pltpu

LRU_C = 8.0
EPS = 1e-6
GELU_K0 = 0.7978845608028654
GELU_K1 = 0.044715

V7X_SUBLANES = 8
V7X_MXU_DIM = 256
V7X_VMEM_LIMIT_BYTES = 58 * 1024 * 1024
HIST_ROWS = V7X_SUBLANES
PROMPT_TILE_ROWS = 512
SAMPLE_TILE_ROWS = 256
FF_CHUNK = 1024


def _rms_norm(x, g):
    return x * lax.rsqrt(jnp.mean(x * x, axis=-1, keepdims=True) + EPS) * g


def _sigmoid(x):
    return 0.5 * jnp.tanh(0.5 * x) + 0.5


def _gelu_tanh(x):
    return 0.5 * x * (1.0 + jnp.tanh(GELU_K0 * (x + GELU_K1 * (x * x * x))))


def _softplus(x):
    return jnp.maximum(x, 0.0) + jnp.log1p(jnp.exp(-jnp.abs(x)))


def _causal_conv(ext, w_ref, n_seq, seg):
    width, chans = w_ref.shape
    acc = None
    for k in range(width):
        lo = HIST_ROWS - (width - 1) + k
        term = ext[:, lo:lo + seg, :].reshape(n_seq * seg, chans) * w_ref[k:k + 1, :]
        acc = term if acc is None else acc + term
    return acc


def _layer_kernel(
    x_ref, xp_ref, sa_ref, sl_ref, sh_ref,
    g1_ref, w_in_ref, caw_ref, lcw_ref, lcb_ref, wg_ref, ba_ref, bx_ref, ap_ref,
    w_out_ref, g2_ref, w_up_ref, w_down_ref, gf_ref,
    y_ref, na_ref, nl_ref, nh_ref,
    ext_a, ext_l, h_c, a_s, b_s, h_s, oa_s, gl_s,
    *, n_chunks, reset_first):
    n_seq, seg, d_model = x_ref.shape
    rows = n_seq * seg
    d_conv = caw_ref.shape[1]
    d_lru = lcw_ref.shape[1]
    wa = caw_ref.shape[0]
    wl = lcw_ref.shape[0]
    g = pl.program_id(0)
    front_chunk = g % n_chunks
    back_chunk = (g + n_chunks - 1) % n_chunks

    wr = g % 2
    rd = 1 - wr
    a_rd, b_rd, oa_rd, gl_rd = (r.at[rd] for r in (a_s, b_s, oa_s, gl_s))
    a_wr, b_wr, oa_wr, gl_wr = (r.at[wr] for r in (a_s, b_s, oa_s, gl_s))

    @pl.when(g == 0)
    def _():
        for ref in (h_c, a_s, b_s, oa_s, gl_s):
            ref[...] = jnp.zeros(ref.shape, ref.dtype)

    @pl.when(front_chunk == 0)
    def _():
        ext_a[:, HIST_ROWS - (wa - 1):HIST_ROWS, :] = sa_ref[...]
        ext_l[:, HIST_ROWS - (wl - 1):HIST_ROWS, :] = sl_ref[...]

    @pl.when(back_chunk == 0)
    def _():
        h_c[...] = sh_ref[...]


    x = x_ref[...].reshape(rows, d_model)
    hn = _rms_norm(x, g1_ref[...]).astype(jnp.bfloat16)
    proj = jnp.dot(hn, w_in_ref[...], preferred_element_type=jnp.float32)
    g_b = proj[:, 0:d_conv]
    g_c = proj[:, d_conv:2 * d_conv]
    x_a = proj[:, 2 * d_conv:3 * d_conv]
    x_l = proj[:, 3 * d_conv:3 * d_conv + d_lru]
    g_l = proj[:, 3 * d_conv + d_lru:]

    for s in range(n_seq):
        h = h_c[s]
        for t in range(seg):
            r = s * seg + t
            h = a_rd[r:r + 1, :] * h + b_rd[r:r + 1, :]
            h_s[r:r + 1, :] = h
        h_c[s] = h
    nh_ref[...] = h_c[...]

    out_b = h_s[...] * gl_rd[...]
    mix = jnp.concatenate([oa_rd[...], out_b.astype(jnp.bfloat16)], axis=-1)
    x_prev = xp_ref[...].reshape(rows, d_model)
    x1 = x_prev + jnp.dot(mix, w_out_ref[...], preferred_element_type=jnp.float32)
    hm = _rms_norm(x1, g2_ref[...]).astype(jnp.bfloat16)
    d_ff = w_up_ref.shape[1]
    n_ff = d_ff // FF_CHUNK

    def mlp_chunk(f, acc):
        cols = slice(f * FF_CHUNK, (f + 1) * FF_CHUNK)
        up = jnp.dot(hm, w_up_ref[:, cols], preferred_element_type=jnp.float32)
        act = jnp.square(jnp.maximum(up, 0.0)).astype(jnp.bfloat16)
        return acc + jnp.dot(act, w_down_ref[cols, :], preferred_element_type=jnp.float32)

    x2 = x1
    for f in range(n_ff // 2):
        x2 = mlp_chunk(f, x2)

    ext_a[:, HIST_ROWS:, :] = (g_c * x_a).reshape(n_seq, seg, d_conv)
    out_a = g_b * _causal_conv(ext_a, caw_ref, n_seq, seg)
    new_a = ext_a[:, seg + HIST_ROWS - (wa - 1):seg + HIST_ROWS, :]
    na_ref[...] = new_a
    ext_a[:, HIST_ROWS - (wa - 1):HIST_ROWS, :] = new_a

    ext_l[:, HIST_ROWS:, :] = x_l.reshape(n_seq, seg, d_lru)
    c = _causal_conv(ext_l, lcw_ref, n_seq, seg) + lcb_ref[...]
    new_l = ext_l[:, seg + HIST_ROWS - (wl - 1):seg + HIST_ROWS, :]
    nl_ref[...] = new_l
    ext_l[:, HIST_ROWS - (wl - 1):HIST_ROWS, :] = new_l

    c_bf = c.astype(jnp.bfloat16)
    neg_rate = -LRU_C * _softplus(ap_ref[...])
    for p in range(d_lru // V7X_MXU_DIM):
        cols = slice(p * V7X_MXU_DIM, (p + 1) * V7X_MXU_DIM)
        gates = jnp.dot(c_bf[:, cols], wg_ref[p], preferred_element_type=jnp.float32)
        r_gate = _sigmoid(gates[:, :V7X_MXU_DIM] + ba_ref[:, cols])
        i_gate = _sigmoid(gates[:, V7X_MXU_DIM:] + bx_ref[:, cols])
        a = jnp.exp(r_gate * neg_rate[:, cols])
        mult = jnp.sqrt(1.0 - a * a)
        ic = i_gate * c[:, cols]
        a_wr[:, cols] = a
        b_wr[:, cols] = mult * ic
        if reset_first:
            first = front_chunk == 0
            for s in range(n_seq):
                r0 = s * seg
                b_wr[r0:r0 + 1, cols] = jnp.where(
                    first, ic[r0:r0 + 1, :], (mult * ic)[r0:r0 + 1, :])

    oa_wr[...] = out_a.astype(jnp.bfloat16)
    gl_wr[...] = _gelu_tanh(g_l)

    for f in range(n_ff // 2, n_ff):
        x2 = mlp_chunk(f, x2)
    y_ref[...] = _rms_norm(x2, gf_ref[...]).reshape(n_seq, seg, d_model)


def _resident(shape):
    zeros = (0,) * len(shape)
    return pl.BlockSpec(shape, lambda g: zeros, pipeline_mode=pl.Buffered(1))


def _run_layer(x, st_a, st_l, st_h, params, *, seq_per_block, chunk, reset_first, name):
    n_seq, t_len, d_model = x.shape
    d_conv = st_a.shape[-1]
    d_lru = st_l.shape[-1]
    assert n_seq % seq_per_block == 0 and t_len % chunk == 0
    assert chunk % V7X_SUBLANES == 0
    rows = seq_per_block * chunk
    n_chunks = t_len // chunk
    n_tiles = (n_seq // seq_per_block) * n_chunks

    def front_tile(g):
        return jnp.minimum(g, n_tiles - 1)

    def back_tile(g):
        return jnp.maximum(g - 1, 0)

    def tile_spec(tile_of):
        return pl.BlockSpec(
            (seq_per_block, chunk, d_model),
            lambda g: (tile_of(g) // n_chunks, tile_of(g) % n_chunks, 0))

    def per_seq(a, tile_of):
        return pl.BlockSpec((seq_per_block,) + a.shape[1:],
                            lambda g: (tile_of(g) // n_chunks, 0, 0))

    in_specs = [tile_spec(front_tile), tile_spec(back_tile), per_seq(st_a, front_tile),
                per_seq(st_l, front_tile), per_seq(st_h, back_tile)]
    in_specs += [_resident(p.shape) for p in params]
    out_shape = (
        jax.ShapeDtypeStruct(x.shape, jnp.float32),
        jax.ShapeDtypeStruct(st_a.shape, jnp.float32),
        jax.ShapeDtypeStruct(st_l.shape, jnp.float32),
        jax.ShapeDtypeStruct(st_h.shape, jnp.float32),
    )
    out_specs = (tile_spec(back_tile), per_seq(st_a, front_tile),
                 per_seq(st_l, front_tile), per_seq(st_h, back_tile))
    f32 = jnp.float32
    scratch = [
        pltpu.VMEM((seq_per_block, HIST_ROWS + chunk, d_conv), f32),
        pltpu.VMEM((seq_per_block, HIST_ROWS + chunk, d_lru), f32),
        pltpu.VMEM((seq_per_block, 1, d_lru), f32),
        pltpu.VMEM((2, rows, d_lru), f32),
        pltpu.VMEM((2, rows, d_lru), f32),
        pltpu.VMEM((rows, d_lru), f32),
        pltpu.VMEM((2, rows, d_conv), jnp.bfloat16),
        pltpu.VMEM((2, rows, d_lru), f32),
    ]
    return pl.pallas_call(
        functools.partial(_layer_kernel, n_chunks=n_chunks, reset_first=reset_first),
        grid=(n_tiles + 1,),
        in_specs=in_specs,
        out_specs=out_specs,
        out_shape=out_shape,
        scratch_shapes=scratch,
        compiler_params=pltpu.CompilerParams(
            dimension_semantics=("arbitrary",),
            vmem_limit_bytes=V7X_VMEM_LIMIT_BYTES),
        name=name,
    )(x, x, st_a, st_l, st_h, *params)


def _block_diag(w):
    n_heads, d, _ = w.shape
    eye = jnp.eye(n_heads, dtype=w.dtype)
    return (w[:, :, None, :] * eye[:, None, :, None]).reshape(n_heads * d, n_heads * d)


def _gate_weights(wa, wx):
    bd_a, bd_x = _block_diag(wa), _block_diag(wx)
    d_lru = bd_a.shape[0]
    groups = []
    for p in range(d_lru // V7X_MXU_DIM):
        sl = slice(p * V7X_MXU_DIM, (p + 1) * V7X_MXU_DIM)
        groups.append(jnp.concatenate([bd_a[sl, sl], bd_x[sl, sl]], axis=1))
    return jnp.stack(groups).astype(jnp.bfloat16)


def kernel(x_prompt, x_sample, state_conv_a, state_lru_conv, state_lru_h, norm1_g, w_in, conv_a_w, lru_conv_w, lru_conv_b, lru_wa, lru_ba, lru_wx, lru_bx, lru_a_param, w_out, norm2_g, w_up, w_down, norm_f_g):
    depth = w_in.shape[0]
    assert depth == 1
    n_p = x_prompt.shape[0]
    n_s, dec_seq, _ = x_sample.shape
    d_conv = conv_a_w.shape[-1]
    d_lru = lru_conv_w.shape[-1]
    bf16 = jnp.bfloat16
    l = 0
    params = (
        norm1_g[l][None, :], w_in[l].astype(bf16), conv_a_w[l], lru_conv_w[l],
        lru_conv_b[l][None, :], _gate_weights(lru_wa[l], lru_wx[l]),
        lru_ba[l].reshape(1, d_lru), lru_bx[l].reshape(1, d_lru),
        lru_a_param[l][None, :], w_out[l].astype(bf16), norm2_g[l][None, :],
        w_up[l].astype(bf16), w_down[l].astype(bf16), norm_f_g[None, :],
    )
    f32 = jnp.float32
    yp, pa, pl_, ph = _run_layer(
        x_prompt,
        jnp.zeros((n_p, conv_a_w.shape[1] - 1, d_conv), f32),
        jnp.zeros((n_p, lru_conv_w.shape[1] - 1, d_lru), f32),
        jnp.zeros((n_p, 1, d_lru), f32),
        params, seq_per_block=1, chunk=PROMPT_TILE_ROWS, reset_first=True,
        name="layer_prompt")
    ys, sa, sl, sh = _run_layer(
        x_sample, state_conv_a[l], state_lru_conv[l], state_lru_h[l][:, None, :],
        params, seq_per_block=SAMPLE_TILE_ROWS // dec_seq, chunk=dec_seq,
        reset_first=False, name="layer_sample")
    return (yp, ys, pa[None], pl_[None], ph[:, 0, :][None],
            sa[None], sl[None], sh[:, 0, :][None])
```

```python
import functools

import jax
import jax.numpy as jnp
from jax import lax
from jax.experimental import pallas as pl
from jax.experimental.pallas import tpu as pltpu

LRU_C = 8.0
EPS = 1e-6
GELU_K0 = 0.7978845608028654
GELU_K1 = 0.044715

V7X_SUBLANES = 8
V7X_MXU_DIM = 256
V7X_VMEM_LIMIT_BYTES = 58 * 1024 * 1024
HIST_ROWS = V7X_SUBLANES
PROMPT_TILE_ROWS = 512
SAMPLE_TILE_ROWS = 256
FF_CHUNK = 1024
SCAN_BLOCK = 4


def _rms_norm(x, g):
    return x * lax.rsqrt(jnp.mean(x * x, axis=-1, keepdims=True) + EPS) * g


def _sigmoid(x):
    return 0.5 * jnp.tanh(0.5 * x) + 0.5


def _gelu_tanh(x):
    return 0.5 * x * (1.0 + jnp.tanh(GELU_K0 * (x + GELU_K1 * (x * x * x))))


def _softplus(x):
    return jnp.maximum(x, 0.0) + jnp.log1p(jnp.exp(-jnp.abs(x)))


def _causal_conv(ext, w_ref, n_seq, seg):
    width, chans = w_ref.shape
    acc = None
    for k in range(width):
        lo = HIST_ROWS - (width - 1) + k
        term = ext[:, lo:lo + seg, :].reshape(n_seq * seg, chans) * w_ref[k:k + 1, :]
        acc = term if acc is None else acc + term
    return acc


def _layer_kernel(
    x_ref, xp_ref, sa_ref, sl_ref, sh_ref,
    g1_ref, w_in_ref, caw_ref, lcw_ref, lcb_ref, wg_ref, ba_ref, bx_ref, ap_ref,
    w_out_ref, g2_ref, w_up_ref, w_down_ref, gf_ref,
    y_ref, na_ref, nl_ref, nh_ref,
    ext_a, ext_l, h_c, a_s, b_s, h_s, oa_s, gl_s,
    *, n_chunks, reset_first):
    n_seq, seg, d_model = x_ref.shape
    rows = n_seq * seg
    d_conv = caw_ref.shape[1]
    d_lru = lcw_ref.shape[1]
    wa = caw_ref.shape[0]
    wl = lcw_ref.shape[0]
    g = pl.program_id(0)
    front_chunk = g % n_chunks
    back_chunk = (g + n_chunks - 1) % n_chunks

    wr = g % 2
    rd = 1 - wr
    a_rd, b_rd, oa_rd, gl_rd = (r.at[rd] for r in (a_s, b_s, oa_s, gl_s))
    a_wr, b_wr, oa_wr, gl_wr = (r.at[wr] for r in (a_s, b_s, oa_s, gl_s))

    @pl.when(g == 0)
    def _():
        for ref in (h_c, a_s, b_s, oa_s, gl_s):
            ref[...] = jnp.zeros(ref.shape, ref.dtype)

    @pl.when(front_chunk == 0)
    def _():
        ext_a[:, HIST_ROWS - (wa - 1):HIST_ROWS, :] = sa_ref[...]
        ext_l[:, HIST_ROWS - (wl - 1):HIST_ROWS, :] = sl_ref[...]

    @pl.when(back_chunk == 0)
    def _():
        h_c[...] = sh_ref[...]

    scan_state = [None]

    def scan_rows(lo, hi):
        assert seg % SCAN_BLOCK == 0 and lo % SCAN_BLOCK == 0 and hi % SCAN_BLOCK == 0
        for r in range(lo, hi, SCAN_BLOCK):
            s, t = divmod(r, seg)
            if t == 0:
                scan_state[0] = h_c[s]
            h_in = scan_state[0]
            a0, a1, a2, a3 = (a_rd[r + k:r + k + 1, :] for k in range(SCAN_BLOCK))
            b0, b1, b2, b3 = (b_rd[r + k:r + k + 1, :] for k in range(SCAN_BLOCK))
            a01 = a1 * a0
            b01 = a1 * b0 + b1
            a23 = a3 * a2
            b23 = a3 * b2 + b3
            h0 = a0 * h_in + b0
            h1 = a01 * h_in + b01
            h2 = a2 * h1 + b2
            h3 = (a23 * a01) * h_in + (a23 * b01 + b23)
            for k, h in enumerate((h0, h1, h2, h3)):
                h_s[r + k:r + k + 1, :] = h
            scan_state[0] = h3
            if t + SCAN_BLOCK == seg:
                h_c[s] = h3

    def in_proj(lo, hi):
        return jnp.dot(hn, w_in_ref[:, lo:hi], preferred_element_type=jnp.float32)

    def gate_coeffs(p):
        cols = slice(p * V7X_MXU_DIM, (p + 1) * V7X_MXU_DIM)
        r_gate = _sigmoid(gates[p][:, :V7X_MXU_DIM] + ba_ref[:, cols])
        i_gate = _sigmoid(gates[p][:, V7X_MXU_DIM:] + bx_ref[:, cols])
        a = jnp.exp(r_gate * neg_rate[:, cols])
        mult = jnp.sqrt(1.0 - a * a)
        ic = i_gate * c[:, cols]
        a_wr[:, cols] = a
        b_wr[:, cols] = mult * ic
        if reset_first:
            first = front_chunk == 0
            for s in range(n_seq):
                r0 = s * seg
                b_wr[r0:r0 + 1, cols] = jnp.where(
                    first, ic[r0:r0 + 1, :], (mult * ic)[r0:r0 + 1, :])

    def mlp_chunk(f, acc):
        cols = slice(f * FF_CHUNK, (f + 1) * FF_CHUNK)
        up = jnp.dot(hm, w_up_ref[:, cols], preferred_element_type=jnp.float32)
        act = jnp.square(jnp.maximum(up, 0.0)).astype(jnp.bfloat16)
        return acc + jnp.dot(act, w_down_ref[cols, :], preferred_element_type=jnp.float32)

    half = rows // 2
    d_in = w_in_ref.shape[1]
    x = x_ref[...].reshape(rows, d_model)
    hn = _rms_norm(x, g1_ref[...]).astype(jnp.bfloat16)
    proj_lo = in_proj(0, d_in // 2)
    scan_rows(0, half)
    proj_hi = in_proj(d_in // 2, d_in)
    scan_rows(half, rows)
    nh_ref[...] = h_c[...]
    proj = jnp.concatenate([proj_lo, proj_hi], axis=-1)
    g_b = proj[:, 0:d_conv]
    g_c = proj[:, d_conv:2 * d_conv]
    x_a = proj[:, 2 * d_conv:3 * d_conv]
    x_l = proj[:, 3 * d_conv:3 * d_conv + d_lru]
    g_l = proj[:, 3 * d_conv + d_lru:]

    out_b = h_s[...] * gl_rd[...]
    mix = jnp.concatenate([oa_rd[...], out_b.astype(jnp.bfloat16)], axis=-1)
    x_prev = xp_ref[...].reshape(rows, d_model)
    x1 = x_prev + jnp.dot(mix, w_out_ref[...], preferred_element_type=jnp.float32)
    hm = _rms_norm(x1, g2_ref[...]).astype(jnp.bfloat16)
    n_ff = w_up_ref.shape[1] // FF_CHUNK
    x2 = x1
    for f in range(n_ff // 2):
        x2 = mlp_chunk(f, x2)

    ext_a[:, HIST_ROWS:, :] = (g_c * x_a).reshape(n_seq, seg, d_conv)
    out_a = g_b * _causal_conv(ext_a, caw_ref, n_seq, seg)
    new_a = ext_a[:, seg + HIST_ROWS - (wa - 1):seg + HIST_ROWS, :]
    na_ref[...] = new_a
    ext_a[:, HIST_ROWS - (wa - 1):HIST_ROWS, :] = new_a

    ext_l[:, HIST_ROWS:, :] = x_l.reshape(n_seq, seg, d_lru)
    c = _causal_conv(ext_l, lcw_ref, n_seq, seg) + lcb_ref[...]
    new_l = ext_l[:, seg + HIST_ROWS - (wl - 1):seg + HIST_ROWS, :]
    nl_ref[...] = new_l
    ext_l[:, HIST_ROWS - (wl - 1):HIST_ROWS, :] = new_l

    c_bf = c.astype(jnp.bfloat16)
    neg_rate = -LRU_C * _softplus(ap_ref[...])
    n_groups = d_lru // V7X_MXU_DIM
    gates = [jnp.dot(c_bf[:, p * V7X_MXU_DIM:(p + 1) * V7X_MXU_DIM], wg_ref[p],
                     preferred_element_type=jnp.float32) for p in range(n_groups)]
    for p in range(n_groups):
        gate_coeffs(p)
    oa_wr[...] = out_a.astype(jnp.bfloat16)
    gl_wr[...] = _gelu_tanh(g_l)

    for f in range(n_ff // 2, n_ff):
        x2 = mlp_chunk(f, x2)
    y_ref[...] = _rms_norm(x2, gf_ref[...]).reshape(n_seq, seg, d_model)


def _resident(shape):
    zeros = (0,) * len(shape)
    return pl.BlockSpec(shape, lambda g: zeros, pipeline_mode=pl.Buffered(1))


def _run_layer(x, st_a, st_l, st_h, params, *, seq_per_block, chunk, reset_first, name):
    n_seq, t_len, d_model = x.shape
    d_conv = st_a.shape[-1]
    d_lru = st_l.shape[-1]
    assert n_seq % seq_per_block == 0 and t_len % chunk == 0
    assert chunk % V7X_SUBLANES == 0
    rows = seq_per_block * chunk
    n_chunks = t_len // chunk
    n_tiles = (n_seq // seq_per_block) * n_chunks

    def front_tile(g):
        return jnp.minimum(g, n_tiles - 1)

    def back_tile(g):
        return jnp.maximum(g - 1, 0)

    def tile_spec(tile_of):
        return pl.BlockSpec(
            (seq_per_block, chunk, d_model),
            lambda g: (tile_of(g) // n_chunks, tile_of(g) % n_chunks, 0))

    def per_seq(a, tile_of):
        return pl.BlockSpec((seq_per_block,) + a.shape[1:],
                            lambda g: (tile_of(g) // n_chunks, 0, 0))

    in_specs = [tile_spec(front_tile), tile_spec(back_tile), per_seq(st_a, front_tile),
                per_seq(st_l, front_tile), per_seq(st_h, back_tile)]
    in_specs += [_resident(p.shape) for p in params]
    out_shape = (
        jax.ShapeDtypeStruct(x.shape, jnp.float32),
        jax.ShapeDtypeStruct(st_a.shape, jnp.float32),
        jax.ShapeDtypeStruct(st_l.shape, jnp.float32),
        jax.ShapeDtypeStruct(st_h.shape, jnp.float32),
    )
    out_specs = (tile_spec(back_tile), per_seq(st_a, front_tile),
                 per_seq(st_l, front_tile), per_seq(st_h, back_tile))
    f32 = jnp.float32
    scratch = [
        pltpu.VMEM((seq_per_block, HIST_ROWS + chunk, d_conv), f32),
        pltpu.VMEM((seq_per_block, HIST_ROWS + chunk, d_lru), f32),
        pltpu.VMEM((seq_per_block, 1, d_lru), f32),
        pltpu.VMEM((2, rows, d_lru), f32),
        pltpu.VMEM((2, rows, d_lru), f32),
        pltpu.VMEM((rows, d_lru), f32),
        pltpu.VMEM((2, rows, d_conv), jnp.bfloat16),
        pltpu.VMEM((2, rows, d_lru), f32),
    ]
    return pl.pallas_call(
        functools.partial(_layer_kernel, n_chunks=n_chunks, reset_first=reset_first),
        grid=(n_tiles + 1,),
        in_specs=in_specs,
        out_specs=out_specs,
        out_shape=out_shape,
        scratch_shapes=scratch,
        compiler_params=pltpu.CompilerParams(
            dimension_semantics=("arbitrary",),
            vmem_limit_bytes=V7X_VMEM_LIMIT_BYTES),
        name=name,
    )(x, x, st_a, st_l, st_h, *params)


def _block_diag(w):
    n_heads, d, _ = w.shape
    eye = jnp.eye(n_heads, dtype=w.dtype)
    return (w[:, :, None, :] * eye[:, None, :, None]).reshape(n_heads * d, n_heads * d)


def _gate_weights(wa, wx):
    bd_a, bd_x = _block_diag(wa), _block_diag(wx)
    d_lru = bd_a.shape[0]
    groups = []
    for p in range(d_lru // V7X_MXU_DIM):
        sl = slice(p * V7X_MXU_DIM, (p + 1) * V7X_MXU_DIM)
        groups.append(jnp.concatenate([bd_a[sl, sl], bd_x[sl, sl]], axis=1))
    return jnp.stack(groups).astype(jnp.bfloat16)


def kernel(x_prompt, x_sample, state_conv_a, state_lru_conv, state_lru_h, norm1_g, w_in, conv_a_w, lru_conv_w, lru_conv_b, lru_wa, lru_ba, lru_wx, lru_bx, lru_a_param, w_out, norm2_g, w_up, w_down, norm_f_g):
    depth = w_in.shape[0]
    assert depth == 1
    n_p = x_prompt.shape[0]
    n_s, dec_seq, _ = x_sample.shape
    d_conv = conv_a_w.shape[-1]
    d_lru = lru_conv_w.shape[-1]
    bf16 = jnp.bfloat16
    l = 0
    params = (
        norm1_g[l][None, :], w_in[l].astype(bf16), conv_a_w[l], lru_conv_w[l],
        lru_conv_b[l][None, :], _gate_weights(lru_wa[l], lru_wx[l]),
        lru_ba[l].reshape(1, d_lru), lru_bx[l].reshape(1, d_lru),
        lru_a_param[l][None, :], w_out[l].astype(bf16), norm2_g[l][None, :],
        w_up[l].astype(bf16), w_down[l].astype(bf16), norm_f_g[None, :],
    )
    f32 = jnp.float32
    yp, pa, pl_, ph = _run_layer(
        x_prompt,
        jnp.zeros((n_p, conv_a_w.shape[1] - 1, d_conv), f32),
        jnp.zeros((n_p, lru_conv_w.shape[1] - 1, d_lru), f32),
        jnp.zeros((n_p, 1, d_lru), f32),
        params, seq_per_block=1, chunk=PROMPT_TILE_ROWS, reset_first=True,
        name="layer_prompt")
    ys, sa, sl, sh = _run_layer(
        x_sample, state_conv_a[l], state_lru_conv[l], state_lru_h[l][:, None, :],
        params, seq_per_block=SAMPLE_TILE_ROWS // dec_seq, chunk=dec_seq,
        reset_first=False, name="layer_sample")
    return (yp, ys, pa[None], pl_[None], ph[:, 0, :][None],
            sa[None], sl[None], sh[:, 0, :][None])
```

```python
import functools

import jax
import jax.numpy as jnp
from jax import lax
from jax.experimental import pallas as pl
from jax.experimental.pallas import tpu as pltpu

LRU_C = 8.0
EPS = 1e-6
GELU_K0 = 0.7978845608028654
GELU_K1 = 0.044715

V7X_SUBLANES = 8
V7X_MXU_DIM = 256
V7X_VMEM_LIMIT_BYTES = 58 * 1024 * 1024
HIST_ROWS = V7X_SUBLANES
PROMPT_TILE_ROWS = 512
SAMPLE_TILE_ROWS = 512
FF_CHUNK = 1024
SCAN_BLOCK = 4


def _rms_norm(x, g):
    return x * lax.rsqrt(jnp.mean(x * x, axis=-1, keepdims=True) + EPS) * g


def _sigmoid(x):
    return 0.5 * jnp.tanh(0.5 * x) + 0.5


def _gelu_tanh(x):
    return 0.5 * x * (1.0 + jnp.tanh(GELU_K0 * (x + GELU_K1 * (x * x * x))))


def _softplus(x):
    return jnp.maximum(x, 0.0) + jnp.log1p(jnp.exp(-jnp.abs(x)))


def _causal_conv(ext, w_ref, n_seq, seg):
    width, chans = w_ref.shape
    acc = None
    for k in range(width):
        lo = HIST_ROWS - (width - 1) + k
        term = ext[:, lo:lo + seg, :].reshape(n_seq * seg, chans) * w_ref[k:k + 1, :]
        acc = term if acc is None else acc + term
    return acc


def _layer_kernel(
    x_ref, xp_ref, sa_ref, sl_ref, sh_ref,
    g1_ref, w_in_ref, caw_ref, lcw_ref, lcb_ref, wg_ref, ba_ref, bx_ref, ap_ref,
    w_out_ref, g2_ref, w_up_ref, w_down_ref, gf_ref,
    y_ref, na_ref, nl_ref, nh_ref,
    ext_a, ext_l, h_c, a_s, b_s, h_s, oa_s, gl_s,
    *, n_chunks, reset_first, skewed):
    n_seq, seg, d_model = x_ref.shape
    rows = n_seq * seg
    d_conv = caw_ref.shape[1]
    d_lru = lcw_ref.shape[1]
    wa = caw_ref.shape[0]
    wl = lcw_ref.shape[0]
    n_groups = d_lru // V7X_MXU_DIM
    n_ff = w_up_ref.shape[1] // FF_CHUNK
    g = pl.program_id(0)
    front_chunk = g % n_chunks

    if skewed:
        back_chunk = (g + n_chunks - 1) % n_chunks
        wr = g % 2
        rd = 1 - wr

        @pl.when(g == 0)
        def _():
            for ref in (h_c, a_s, b_s, oa_s, gl_s):
                ref[...] = jnp.zeros(ref.shape, ref.dtype)
    else:
        back_chunk = front_chunk
        wr = rd = 0
    a_rd, b_rd, oa_rd, gl_rd = (r.at[rd] for r in (a_s, b_s, oa_s, gl_s))
    a_wr, b_wr, oa_wr, gl_wr = (r.at[wr] for r in (a_s, b_s, oa_s, gl_s))

    @pl.when(front_chunk == 0)
    def _():
        ext_a[:, HIST_ROWS - (wa - 1):HIST_ROWS, :] = sa_ref[...]
        ext_l[:, HIST_ROWS - (wl - 1):HIST_ROWS, :] = sl_ref[...]

    @pl.when(back_chunk == 0)
    def _():
        h_c[...] = sh_ref[...]

    def front_project():
        x = x_ref[...].reshape(rows, d_model)
        hn = _rms_norm(x, g1_ref[...]).astype(jnp.bfloat16)
        return jnp.dot(hn, w_in_ref[...], preferred_element_type=jnp.float32)

    def front_mix(proj):
        g_b = proj[:, 0:d_conv]
        g_c = proj[:, d_conv:2 * d_conv]
        x_a = proj[:, 2 * d_conv:3 * d_conv]
        x_l = proj[:, 3 * d_conv:3 * d_conv + d_lru]
        g_l = proj[:, 3 * d_conv + d_lru:]

        ext_a[:, HIST_ROWS:, :] = (g_c * x_a).reshape(n_seq, seg, d_conv)
        out_a = g_b * _causal_conv(ext_a, caw_ref, n_seq, seg)
        new_a = ext_a[:, seg + HIST_ROWS - (wa - 1):seg + HIST_ROWS, :]
        na_ref[...] = new_a
        ext_a[:, HIST_ROWS - (wa - 1):HIST_ROWS, :] = new_a

        ext_l[:, HIST_ROWS:, :] = x_l.reshape(n_seq, seg, d_lru)
        c = _causal_conv(ext_l, lcw_ref, n_seq, seg) + lcb_ref[...]
        new_l = ext_l[:, seg + HIST_ROWS - (wl - 1):seg + HIST_ROWS, :]
        nl_ref[...] = new_l
        ext_l[:, HIST_ROWS - (wl - 1):HIST_ROWS, :] = new_l

        c_bf = c.astype(jnp.bfloat16)
        neg_rate = -LRU_C * _softplus(ap_ref[...])
        for p in range(n_groups):
            cols = slice(p * V7X_MXU_DIM, (p + 1) * V7X_MXU_DIM)
            gates = jnp.dot(c_bf[:, cols], wg_ref[p], preferred_element_type=jnp.float32)
            r_gate = _sigmoid(gates[:, :V7X_MXU_DIM] + ba_ref[:, cols])
            i_gate = _sigmoid(gates[:, V7X_MXU_DIM:] + bx_ref[:, cols])
            a = jnp.exp(r_gate * neg_rate[:, cols])
            mult = jnp.sqrt(1.0 - a * a)
            ic = i_gate * c[:, cols]
            a_wr[:, cols] = a
            b_wr[:, cols] = mult * ic
            if reset_first:
                first = front_chunk == 0
                for s in range(n_seq):
                    r0 = s * seg
                    b_wr[r0:r0 + 1, cols] = jnp.where(
                        first, ic[r0:r0 + 1, :], (mult * ic)[r0:r0 + 1, :])
        oa_wr[...] = out_a.astype(jnp.bfloat16)
        gl_wr[...] = _gelu_tanh(g_l)

    def back_scan():
        assert seg % SCAN_BLOCK == 0
        h_in = None
        for r in range(0, rows, SCAN_BLOCK):
            s, t = divmod(r, seg)
            if t == 0:
                h_in = h_c[s]
            a0, a1, a2, a3 = (a_rd[r + k:r + k + 1, :] for k in range(SCAN_BLOCK))
            b0, b1, b2, b3 = (b_rd[r + k:r + k + 1, :] for k in range(SCAN_BLOCK))
            a01 = a1 * a0
            b01 = a1 * b0 + b1
            a23 = a3 * a2
            b23 = a3 * b2 + b3
            h0 = a0 * h_in + b0
            h1 = a01 * h_in + b01
            h2 = a2 * h1 + b2
            h3 = (a23 * a01) * h_in + (a23 * b01 + b23)
            for k, h in enumerate((h0, h1, h2, h3)):
                h_s[r + k:r + k + 1, :] = h
            h_in = h3
            if t + SCAN_BLOCK == seg:
                h_c[s] = h3
        nh_ref[...] = h_c[...]

    def back_project():
        out_b = h_s[...] * gl_rd[...]
        mix = jnp.concatenate([oa_rd[...], out_b.astype(jnp.bfloat16)], axis=-1)
        x_prev = xp_ref[...].reshape(rows, d_model)
        x1 = x_prev + jnp.dot(mix, w_out_ref[...], preferred_element_type=jnp.float32)
        return x1, _rms_norm(x1, g2_ref[...]).astype(jnp.bfloat16)

    def back_mlp(acc, hm, f_lo, f_hi):
        for f in range(f_lo, f_hi):
            cols = slice(f * FF_CHUNK, (f + 1) * FF_CHUNK)
            up = jnp.dot(hm, w_up_ref[:, cols], preferred_element_type=jnp.float32)
            act = jnp.square(jnp.maximum(up, 0.0)).astype(jnp.bfloat16)
            acc = acc + jnp.dot(act, w_down_ref[cols, :], preferred_element_type=jnp.float32)
        return acc

    if skewed:
        proj = front_project()
        back_scan()
        x1, hm = back_project()
        x2 = back_mlp(x1, hm, 0, n_ff // 2)
        front_mix(proj)
        x2 = back_mlp(x2, hm, n_ff // 2, n_ff)
    else:
        front_mix(front_project())
        back_scan()
        x1, hm = back_project()
        x2 = back_mlp(x1, hm, 0, n_ff)
    y_ref[...] = _rms_norm(x2, gf_ref[...]).reshape(n_seq, seg, d_model)


def _resident(shape):
    zeros = (0,) * len(shape)
    return pl.BlockSpec(shape, lambda g: zeros, pipeline_mode=pl.Buffered(1))


def _run_layer(x, st_a, st_l, st_h, params, *, seq_per_block, chunk, reset_first, skewed, name):
    n_seq, t_len, d_model = x.shape
    d_conv = st_a.shape[-1]
    d_lru = st_l.shape[-1]
    assert n_seq % seq_per_block == 0 and t_len % chunk == 0
    assert chunk % V7X_SUBLANES == 0
    rows = seq_per_block * chunk
    n_chunks = t_len // chunk
    n_tiles = (n_seq // seq_per_block) * n_chunks

    def front_tile(g):
        return jnp.minimum(g, n_tiles - 1)

    def back_tile(g):
        return jnp.maximum(g - 1, 0) if skewed else g

    def tile_spec(tile_of):
        return pl.BlockSpec(
            (seq_per_block, chunk, d_model),
            lambda g: (tile_of(g) // n_chunks, tile_of(g) % n_chunks, 0))

    def per_seq(a, tile_of):
        return pl.BlockSpec((seq_per_block,) + a.shape[1:],
                            lambda g: (tile_of(g) // n_chunks, 0, 0))

    in_specs = [tile_spec(front_tile), tile_spec(back_tile), per_seq(st_a, front_tile),
                per_seq(st_l, front_tile), per_seq(st_h, back_tile)]
    in_specs += [_resident(p.shape) for p in params]
    out_shape = (
        jax.ShapeDtypeStruct(x.shape, jnp.float32),
        jax.ShapeDtypeStruct(st_a.shape, jnp.float32),
        jax.ShapeDtypeStruct(st_l.shape, jnp.float32),
        jax.ShapeDtypeStruct(st_h.shape, jnp.float32),
    )
    out_specs = (tile_spec(back_tile), per_seq(st_a, front_tile),
                 per_seq(st_l, front_tile), per_seq(st_h, back_tile))
    f32 = jnp.float32
    n_slots = 2 if skewed else 1
    scratch = [
        pltpu.VMEM((seq_per_block, HIST_ROWS + chunk, d_conv), f32),
        pltpu.VMEM((seq_per_block, HIST_ROWS + chunk, d_lru), f32),
        pltpu.VMEM((seq_per_block, 1, d_lru), f32),
        pltpu.VMEM((n_slots, rows, d_lru), f32),
        pltpu.VMEM((n_slots, rows, d_lru), f32),
        pltpu.VMEM((rows, d_lru), f32),
        pltpu.VMEM((n_slots, rows, d_conv), jnp.bfloat16),
        pltpu.VMEM((n_slots, rows, d_lru), f32),
    ]
    return pl.pallas_call(
        functools.partial(_layer_kernel, n_chunks=n_chunks, reset_first=reset_first,
                          skewed=skewed),
        grid=(n_tiles + 1 if skewed else n_tiles,),
        in_specs=in_specs,
        out_specs=out_specs,
        out_shape=out_shape,
        scratch_shapes=scratch,
        compiler_params=pltpu.CompilerParams(
            dimension_semantics=("arbitrary",),
            vmem_limit_bytes=V7X_VMEM_LIMIT_BYTES),
        name=name,
    )(x, x, st_a, st_l, st_h, *params)


def _block_diag(w):
    n_heads, d, _ = w.shape
    eye = jnp.eye(n_heads, dtype=w.dtype)
    return (w[:, :, None, :] * eye[:, None, :, None]).reshape(n_heads * d, n_heads * d)


def _gate_weights(wa, wx):
    bd_a, bd_x = _block_diag(wa), _block_diag(wx)
    d_lru = bd_a.shape[0]
    groups = []
    for p in range(d_lru // V7X_MXU_DIM):
        sl = slice(p * V7X_MXU_DIM, (p + 1) * V7X_MXU_DIM)
        groups.append(jnp.concatenate([bd_a[sl, sl], bd_x[sl, sl]], axis=1))
    return jnp.stack(groups).astype(jnp.bfloat16)


def kernel(x_prompt, x_sample, state_conv_a, state_lru_conv, state_lru_h, norm1_g, w_in, conv_a_w, lru_conv_w, lru_conv_b, lru_wa, lru_ba, lru_wx, lru_bx, lru_a_param, w_out, norm2_g, w_up, w_down, norm_f_g):
    depth = w_in.shape[0]
    assert depth == 1
    n_p = x_prompt.shape[0]
    n_s, dec_seq, _ = x_sample.shape
    d_conv = conv_a_w.shape[-1]
    d_lru = lru_conv_w.shape[-1]
    bf16 = jnp.bfloat16
    l = 0
    params = (
        norm1_g[l][None, :], w_in[l].astype(bf16), conv_a_w[l], lru_conv_w[l],
        lru_conv_b[l][None, :], _gate_weights(lru_wa[l], lru_wx[l]),
        lru_ba[l].reshape(1, d_lru), lru_bx[l].reshape(1, d_lru),
        lru_a_param[l][None, :], w_out[l].astype(bf16), norm2_g[l][None, :],
        w_up[l].astype(bf16), w_down[l].astype(bf16), norm_f_g[None, :],
    )
    f32 = jnp.float32
    yp, pa, pl_, ph = _run_layer(
        x_prompt,
        jnp.zeros((n_p, conv_a_w.shape[1] - 1, d_conv), f32),
        jnp.zeros((n_p, lru_conv_w.shape[1] - 1, d_lru), f32),
        jnp.zeros((n_p, 1, d_lru), f32),
        params, seq_per_block=1, chunk=PROMPT_TILE_ROWS, reset_first=True,
        skewed=True, name="layer_prompt")
    ys, sa, sl, sh = _run_layer(
        x_sample, state_conv_a[l], state_lru_conv[l], state_lru_h[l][:, None, :],
        params, seq_per_block=SAMPLE_TILE_ROWS // dec_seq, chunk=dec_seq,
        reset_first=False, skewed=False, name="layer_sample")
    return (yp, ys, pa[None], pl_[None], ph[:, 0, :][None],
            sa[None], sl[None], sh[:, 0, :][None])
```

```python
import functools

import jax
import jax.numpy as jnp
from jax import lax
from jax.experimental import pallas as pl
from jax.experimental.pallas import tpu as pltpu

LRU_C = 8.0
EPS = 1e-6
GELU_K0 = 0.7978845608028654
GELU_K1 = 0.044715

V7X_SUBLANES = 8
V7X_MXU_DIM = 256
V7X_VMEM_LIMIT_BYTES = 58 * 1024 * 1024
HIST_ROWS = V7X_SUBLANES
PROMPT_TILE_ROWS = 512
SAMPLE_TILE_ROWS = 512
FF_CHUNK = 1024
SCAN_BLOCK = 4


def _rms_norm(x, g):
    return x * lax.rsqrt(jnp.mean(x * x, axis=-1, keepdims=True) + EPS) * g


def _sigmoid(x):
    return 0.5 * jnp.tanh(0.5 * x) + 0.5


def _gelu_tanh(x):
    return 0.5 * x * (1.0 + jnp.tanh(GELU_K0 * (x + GELU_K1 * (x * x * x))))


def _softplus(x):
    return jnp.maximum(x, 0.0) + jnp.log1p(jnp.exp(-jnp.abs(x)))


def _causal_conv(ext, p_ref, w_row, width, n_seq, seg):
    chans = ext.shape[-1]
    acc = None
    for k in range(width):
        lo = HIST_ROWS - (width - 1) + k
        tap = p_ref[w_row + k:w_row + k + 1, :]
        term = ext[:, lo:lo + seg, :].reshape(n_seq * seg, chans) * tap
        acc = term if acc is None else acc + term
    return acc


def _layer_kernel(*refs, n_chunks, n_tiles, conv_widths, reset_first, skewed, zero_state,
                  blocked_state):
    refs = list(refs)
    x_ref, xp_ref = refs[:2]
    del refs[:2]
    if not zero_state:
        sa_ref, sl_ref, sh_ref = refs[:3]
        del refs[:3]
    (g1_ref, w_in_ref, p_ref, wg_ref, w_out_ref, g2_ref, w_up_ref, w_down_ref, gf_ref,
     y_ref, na_ref, nl_ref, nh_ref,
     ext_a, ext_l, h_c, a_s, b_s, h_s, oa_s, gl_s) = refs
    n_seq, seg, d_model = x_ref.shape
    rows = n_seq * seg
    d_conv = ext_a.shape[-1]
    d_lru = ext_l.shape[-1]
    wa, wl = conv_widths
    caw_row, lcw_row = 0, wa
    lcb_row, ba_row, bx_row, ap_row = (wa + wl + i for i in range(4))
    n_groups = d_lru // V7X_MXU_DIM
    n_ff = w_up_ref.shape[1] // FF_CHUNK
    g = pl.program_id(0)
    front_tile = jnp.minimum(g, n_tiles - 1)
    back_tile = jnp.maximum(g - 1, 0) if skewed else g
    front_chunk = front_tile % n_chunks
    back_chunk = back_tile % n_chunks
    front_seq0 = 0 if blocked_state else (front_tile // n_chunks) * n_seq
    back_seq0 = 0 if blocked_state else (back_tile // n_chunks) * n_seq

    if skewed:
        wr = g % 2
        rd = 1 - wr

        @pl.when(g == 0)
        def _():
            for ref in (h_c, a_s, b_s, oa_s, gl_s):
                ref[...] = jnp.zeros(ref.shape, ref.dtype)
    else:
        wr = rd = 0
    a_rd, b_rd, oa_rd, gl_rd = (r.at[rd] for r in (a_s, b_s, oa_s, gl_s))
    a_wr, b_wr, oa_wr, gl_wr = (r.at[wr] for r in (a_s, b_s, oa_s, gl_s))

    @pl.when(front_chunk == 0)
    def _():
        for k in range(wa - 1):
            row = jnp.zeros((n_seq, d_conv), jnp.float32) if zero_state else sa_ref[k]
            ext_a[:, HIST_ROWS - (wa - 1) + k, :] = row
        for k in range(wl - 1):
            row = jnp.zeros((n_seq, d_lru), jnp.float32) if zero_state else sl_ref[k]
            ext_l[:, HIST_ROWS - (wl - 1) + k, :] = row

    @pl.when(back_chunk == 0)
    def _():
        h_c[...] = jnp.zeros(h_c.shape, jnp.float32) if zero_state else sh_ref[...]

    def front_project():
        x = x_ref[...].reshape(rows, d_model)
        hn = _rms_norm(x, g1_ref[...]).astype(jnp.bfloat16)
        return jnp.dot(hn, w_in_ref[...], preferred_element_type=jnp.float32)

    def front_mix(proj):
        g_b = proj[:, 0:d_conv]
        g_c = proj[:, d_conv:2 * d_conv]
        x_a = proj[:, 2 * d_conv:3 * d_conv]
        x_l = proj[:, 3 * d_conv:3 * d_conv + d_lru]
        g_l = proj[:, 3 * d_conv + d_lru:]

        ext_a[:, HIST_ROWS:, :] = (g_c * x_a).reshape(n_seq, seg, d_conv)
        out_a = g_b * _causal_conv(ext_a, p_ref, caw_row, wa, n_seq, seg)
        for k in range(wa - 1):
            na_ref[k, pl.ds(front_seq0, n_seq), :] = ext_a[:, seg + HIST_ROWS - (wa - 1) + k, :]
        ext_a[:, HIST_ROWS - (wa - 1):HIST_ROWS, :] = (
            ext_a[:, seg + HIST_ROWS - (wa - 1):seg + HIST_ROWS, :])

        ext_l[:, HIST_ROWS:, :] = x_l.reshape(n_seq, seg, d_lru)
        c = (_causal_conv(ext_l, p_ref, lcw_row, wl, n_seq, seg)
             + p_ref[lcb_row:lcb_row + 1, :])
        for k in range(wl - 1):
            nl_ref[k, pl.ds(front_seq0, n_seq), :] = ext_l[:, seg + HIST_ROWS - (wl - 1) + k, :]
        ext_l[:, HIST_ROWS - (wl - 1):HIST_ROWS, :] = (
            ext_l[:, seg + HIST_ROWS - (wl - 1):seg + HIST_ROWS, :])

        c_bf = c.astype(jnp.bfloat16)
        neg_rate = -LRU_C * _softplus(p_ref[ap_row:ap_row + 1, :])
        for p in range(n_groups):
            cols = slice(p * V7X_MXU_DIM, (p + 1) * V7X_MXU_DIM)
            gates = jnp.dot(c_bf[:, cols], wg_ref[p], preferred_element_type=jnp.float32)
            r_gate = _sigmoid(gates[:, :V7X_MXU_DIM] + p_ref[ba_row:ba_row + 1, cols])
            i_gate = _sigmoid(gates[:, V7X_MXU_DIM:] + p_ref[bx_row:bx_row + 1, cols])
            a = jnp.exp(r_gate * neg_rate[:, cols])
            mult = jnp.sqrt(1.0 - a * a)
            ic = i_gate * c[:, cols]
            a_wr[:, cols] = a
            b_wr[:, cols] = mult * ic
            if reset_first:
                first = front_chunk == 0
                for s in range(n_seq):
                    r0 = s * seg
                    b_wr[r0:r0 + 1, cols] = jnp.where(
                        first, ic[r0:r0 + 1, :], (mult * ic)[r0:r0 + 1, :])
        oa_wr[...] = out_a.astype(jnp.bfloat16)
        gl_wr[...] = _gelu_tanh(g_l)

    def back_scan():
        assert seg % SCAN_BLOCK == 0
        h_in = None
        for r in range(0, rows, SCAN_BLOCK):
            s, t = divmod(r, seg)
            if t == 0:
                h_in = h_c[s:s + 1, :]
            a0, a1, a2, a3 = (a_rd[r + k:r + k + 1, :] for k in range(SCAN_BLOCK))
            b0, b1, b2, b3 = (b_rd[r + k:r + k + 1, :] for k in range(SCAN_BLOCK))
            a01 = a1 * a0
            b01 = a1 * b0 + b1
            a23 = a3 * a2
            b23 = a3 * b2 + b3
            h0 = a0 * h_in + b0
            h1 = a01 * h_in + b01
            h2 = a2 * h1 + b2
            h3 = (a23 * a01) * h_in + (a23 * b01 + b23)
            for k, h in enumerate((h0, h1, h2, h3)):
                h_s[r + k:r + k + 1, :] = h
            h_in = h3
            if t + SCAN_BLOCK == seg:
                h_c[s:s + 1, :] = h3
        nh_ref[pl.ds(back_seq0, n_seq), :] = h_c[...]

    def back_project():
        out_b = h_s[...] * gl_rd[...]
        mix = jnp.concatenate([oa_rd[...], out_b.astype(jnp.bfloat16)], axis=-1)
        x_prev = xp_ref[...].reshape(rows, d_model)
        x1 = x_prev + jnp.dot(mix, w_out_ref[...], preferred_element_type=jnp.float32)
        return x1, _rms_norm(x1, g2_ref[...]).astype(jnp.bfloat16)

    def back_mlp(acc, hm, f_lo, f_hi):
        for f in range(f_lo, f_hi):
            cols = slice(f * FF_CHUNK, (f + 1) * FF_CHUNK)
            up = jnp.dot(hm, w_up_ref[:, cols], preferred_element_type=jnp.float32)
            act = jnp.square(jnp.maximum(up, 0.0)).astype(jnp.bfloat16)
            acc = acc + jnp.dot(act, w_down_ref[cols, :], preferred_element_type=jnp.float32)
        return acc

    if skewed:
        proj = front_project()
        back_scan()
        x1, hm = back_project()
        x2 = back_mlp(x1, hm, 0, n_ff // 2)
        front_mix(proj)
        x2 = back_mlp(x2, hm, n_ff // 2, n_ff)
    else:
        front_mix(front_project())
        back_scan()
        x1, hm = back_project()
        x2 = back_mlp(x1, hm, 0, n_ff)
    y_ref[...] = _rms_norm(x2, gf_ref[...]).reshape(n_seq, seg, d_model)


def _resident(shape):
    zeros = (0,) * len(shape)
    return pl.BlockSpec(shape, lambda g: zeros, pipeline_mode=pl.Buffered(1))


def _run_layer(x, states, weights, *, seq_per_block, chunk, conv_widths, reset_first,
               skewed, name):
    n_seq, t_len, d_model = x.shape
    wa, wl = conv_widths
    chans = weights[2].shape[-1]
    assert n_seq % seq_per_block == 0 and t_len % chunk == 0
    assert chunk % V7X_SUBLANES == 0
    rows = seq_per_block * chunk
    n_chunks = t_len // chunk
    n_tiles = (n_seq // seq_per_block) * n_chunks
    blocked_state = seq_per_block % V7X_SUBLANES == 0
    zero_state = states is None
    assert zero_state or blocked_state

    def front_tile(g):
        return jnp.minimum(g, n_tiles - 1)

    def back_tile(g):
        return jnp.maximum(g - 1, 0) if skewed else g

    def tile_spec(tile_of):
        return pl.BlockSpec(
            (seq_per_block, chunk, d_model),
            lambda g: (tile_of(g) // n_chunks, tile_of(g) % n_chunks, 0))

    def rows_spec(n_rows, tile_of):
        if blocked_state:
            return pl.BlockSpec((n_rows, seq_per_block, chans),
                                lambda g: (0, tile_of(g) // n_chunks, 0))
        return pl.BlockSpec((n_rows, n_seq, chans), lambda g: (0, 0, 0))

    def h_spec(tile_of):
        if blocked_state:
            return pl.BlockSpec((seq_per_block, chans), lambda g: (tile_of(g) // n_chunks, 0))
        return pl.BlockSpec((n_seq, chans), lambda g: (0, 0))

    operands = [x, x]
    in_specs = [tile_spec(front_tile), tile_spec(back_tile)]
    if not zero_state:
        operands += list(states)
        in_specs += [rows_spec(wa - 1, front_tile), rows_spec(wl - 1, front_tile),
                     h_spec(back_tile)]
    operands += list(weights)
    in_specs += [_resident(w.shape) for w in weights]
    f32 = jnp.float32
    out_shape = (
        jax.ShapeDtypeStruct(x.shape, f32),
        jax.ShapeDtypeStruct((wa - 1, n_seq, chans), f32),
        jax.ShapeDtypeStruct((wl - 1, n_seq, chans), f32),
        jax.ShapeDtypeStruct((n_seq, chans), f32),
    )
    out_specs = (tile_spec(back_tile), rows_spec(wa - 1, front_tile),
                 rows_spec(wl - 1, front_tile), h_spec(back_tile))
    n_slots = 2 if skewed else 1
    scratch = [
        pltpu.VMEM((seq_per_block, HIST_ROWS + chunk, chans), f32),
        pltpu.VMEM((seq_per_block, HIST_ROWS + chunk, chans), f32),
        pltpu.VMEM((seq_per_block, chans), f32),
        pltpu.VMEM((n_slots, rows, chans), f32),
        pltpu.VMEM((n_slots, rows, chans), f32),
        pltpu.VMEM((rows, chans), f32),
        pltpu.VMEM((n_slots, rows, chans), jnp.bfloat16),
        pltpu.VMEM((n_slots, rows, chans), f32),
    ]
    return pl.pallas_call(
        functools.partial(
            _layer_kernel, n_chunks=n_chunks, n_tiles=n_tiles, conv_widths=conv_widths,
            reset_first=reset_first, skewed=skewed, zero_state=zero_state,
            blocked_state=blocked_state),
        grid=(n_tiles + 1 if skewed else n_tiles,),
        in_specs=in_specs,
        out_specs=out_specs,
        out_shape=out_shape,
        scratch_shapes=scratch,
        compiler_params=pltpu.CompilerParams(
            dimension_semantics=("arbitrary",),
            vmem_limit_bytes=V7X_VMEM_LIMIT_BYTES),
        name=name,
    )(*operands)


def _block_diag(w):
    n_heads, d, _ = w.shape
    eye = jnp.eye(n_heads, dtype=w.dtype)
    return (w[:, :, None, :] * eye[:, None, :, None]).reshape(n_heads * d, n_heads * d)


def _gate_weights(wa, wx):
    bd_a, bd_x = _block_diag(wa), _block_diag(wx)
    d_lru = bd_a.shape[0]
    groups = []
    for p in range(d_lru // V7X_MXU_DIM):
        sl = slice(p * V7X_MXU_DIM, (p + 1) * V7X_MXU_DIM)
        groups.append(jnp.concatenate([bd_a[sl, sl], bd_x[sl, sl]], axis=1))
    return jnp.stack(groups).astype(jnp.bfloat16)


def kernel(x_prompt, x_sample, state_conv_a, state_lru_conv, state_lru_h, norm1_g, w_in, conv_a_w, lru_conv_w, lru_conv_b, lru_wa, lru_ba, lru_wx, lru_bx, lru_a_param, w_out, norm2_g, w_up, w_down, norm_f_g):
    depth = w_in.shape[0]
    assert depth == 1
    dec_seq = x_sample.shape[1]
    d_conv = conv_a_w.shape[-1]
    d_lru = lru_conv_w.shape[-1]
    assert d_conv == d_lru
    conv_widths = (conv_a_w.shape[1], lru_conv_w.shape[1])
    bf16 = jnp.bfloat16
    l = 0
    small = jnp.concatenate([
        conv_a_w[l], lru_conv_w[l], lru_conv_b[l][None, :], lru_ba[l].reshape(1, d_lru),
        lru_bx[l].reshape(1, d_lru), lru_a_param[l][None, :]], axis=0)
    pad_rows = -small.shape[0] % V7X_SUBLANES
    small = jnp.pad(small, ((0, pad_rows), (0, 0)))
    weights = (
        norm1_g[l][None, :], w_in[l].astype(bf16), small,
        _gate_weights(lru_wa[l], lru_wx[l]), w_out[l].astype(bf16), norm2_g[l][None, :],
        w_up[l].astype(bf16), w_down[l].astype(bf16), norm_f_g[None, :],
    )

    def rows_first(a):
        return jnp.transpose(a, (1, 0, 2))

    yp, pa, pl_, ph = _run_layer(
        x_prompt, None, weights, seq_per_block=1, chunk=PROMPT_TILE_ROWS,
        conv_widths=conv_widths, reset_first=True, skewed=True, name="layer_prompt")
    sample_states = (rows_first(state_conv_a[l]), rows_first(state_lru_conv[l]),
                     state_lru_h[l])
    ys, sa, sl, sh = _run_layer(
        x_sample, sample_states, weights, seq_per_block=SAMPLE_TILE_ROWS // dec_seq,
        chunk=dec_seq, conv_widths=conv_widths, reset_first=False, skewed=False,
        name="layer_sample")
    return (yp, ys, rows_first(pa)[None], rows_first(pl_)[None], ph[None],
            rows_first(sa)[None], rows_first(sl)[None], sh[None])
```

```python
import functools

import jax
import jax.numpy as jnp
from jax import lax
from jax.experimental import pallas as pl
from jax.experimental.pallas import tpu as pltpu

LRU_C = 8.0
EPS = 1e-6
GELU_K0 = 0.7978845608028654
GELU_K1 = 0.044715

V7X_SUBLANES = 8
V7X_MXU_DIM = 256
V7X_VMEM_LIMIT_BYTES = 58 * 1024 * 1024
HIST_ROWS = V7X_SUBLANES
PROMPT_TILE_ROWS = 512
SAMPLE_TILE_ROWS = 512
FF_CHUNK = 1024
SCAN_BLOCK = 4


def _rms_norm(x, g):
    return x * lax.rsqrt(jnp.mean(x * x, axis=-1, keepdims=True) + EPS) * g


def _sigmoid(x):
    return 0.5 * jnp.tanh(0.5 * x) + 0.5


def _gelu_tanh(x):
    return 0.5 * x * (1.0 + jnp.tanh(GELU_K0 * (x + GELU_K1 * (x * x * x))))


def _softplus(x):
    return jnp.maximum(x, 0.0) + jnp.log1p(jnp.exp(-jnp.abs(x)))


def _causal_conv(ext, p_ref, w_row, width, n_seq, seg):
    chans = ext.shape[-1]
    acc = None
    for k in range(width):
        lo = HIST_ROWS - (width - 1) + k
        tap = p_ref[w_row + k:w_row + k + 1, :]
        term = ext[:, lo:lo + seg, :].reshape(n_seq * seg, chans) * tap
        acc = term if acc is None else acc + term
    return acc


def _layer_kernel(*refs, n_chunks, n_tiles, conv_widths, reset_first, skewed, zero_state,
                  blocked_state):
    refs = list(refs)
    x_ref, xp_ref = refs[:2]
    del refs[:2]
    if not zero_state:
        sa_ref, sl_ref, sh_ref = refs[:3]
        del refs[:3]
    (g1_ref, w_in_ref, p_ref, wg_ref, w_out_ref, g2_ref, w_up_ref, w_down_ref, gf_ref,
     y_ref, na_ref, nl_ref, nh_ref,
     ext_a, ext_l, h_c, a_s, b_s, h_s, oa_s, gl_s) = refs
    n_seq, seg, d_model = x_ref.shape
    rows = n_seq * seg
    d_conv = ext_a.shape[-1]
    d_lru = ext_l.shape[-1]
    wa, wl = conv_widths
    caw_row, lcw_row = 0, wa
    lcb_row, ba_row, bx_row, ap_row = (wa + wl + i for i in range(4))
    n_groups = d_lru // V7X_MXU_DIM
    n_ff = w_up_ref.shape[1] // FF_CHUNK
    g = pl.program_id(0)
    front_tile = jnp.minimum(g, n_tiles - 1)
    back_tile = jnp.maximum(g - 1, 0) if skewed else g
    front_chunk = front_tile % n_chunks
    back_chunk = back_tile % n_chunks
    front_seq0 = 0 if blocked_state else (front_tile // n_chunks) * n_seq
    back_seq0 = 0 if blocked_state else (back_tile // n_chunks) * n_seq

    if skewed:
        wr = g % 2
        rd = 1 - wr

        @pl.when(g == 0)
        def _():
            for ref in (h_c, a_s, b_s, oa_s, gl_s):
                ref[...] = jnp.zeros(ref.shape, ref.dtype)
    else:
        wr = rd = 0
    a_rd, b_rd, oa_rd, gl_rd = (r.at[rd] for r in (a_s, b_s, oa_s, gl_s))
    a_wr, b_wr, oa_wr, gl_wr = (r.at[wr] for r in (a_s, b_s, oa_s, gl_s))

    @pl.when(front_chunk == 0)
    def _():
        for k in range(wa - 1):
            row = jnp.zeros((n_seq, d_conv), jnp.float32) if zero_state else sa_ref[k]
            ext_a[:, HIST_ROWS - (wa - 1) + k, :] = row
        for k in range(wl - 1):
            row = jnp.zeros((n_seq, d_lru), jnp.float32) if zero_state else sl_ref[k]
            ext_l[:, HIST_ROWS - (wl - 1) + k, :] = row

    @pl.when(back_chunk == 0)
    def _():
        h_c[...] = jnp.zeros(h_c.shape, jnp.float32) if zero_state else sh_ref[...]

    def front_project():
        x = x_ref[...].reshape(rows, d_model)
        hn = _rms_norm(x, g1_ref[...]).astype(jnp.bfloat16)
        return jnp.dot(hn, w_in_ref[...], preferred_element_type=jnp.float32)

    def front_mix(proj):
        g_b = proj[:, 0:d_conv]
        g_c = proj[:, d_conv:2 * d_conv]
        x_a = proj[:, 2 * d_conv:3 * d_conv]
        x_l = proj[:, 3 * d_conv:3 * d_conv + d_lru]
        g_l = proj[:, 3 * d_conv + d_lru:]

        ext_a[:, HIST_ROWS:, :] = (g_c * x_a).reshape(n_seq, seg, d_conv)
        out_a = g_b * _causal_conv(ext_a, p_ref, caw_row, wa, n_seq, seg)
        for k in range(wa - 1):
            na_ref[k, pl.ds(front_seq0, n_seq), :] = ext_a[:, seg + HIST_ROWS - (wa - 1) + k, :]
        ext_a[:, HIST_ROWS - (wa - 1):HIST_ROWS, :] = (
            ext_a[:, seg + HIST_ROWS - (wa - 1):seg + HIST_ROWS, :])

        ext_l[:, HIST_ROWS:, :] = x_l.reshape(n_seq, seg, d_lru)
        c = (_causal_conv(ext_l, p_ref, lcw_row, wl, n_seq, seg)
             + p_ref[lcb_row:lcb_row + 1, :])
        for k in range(wl - 1):
            nl_ref[k, pl.ds(front_seq0, n_seq), :] = ext_l[:, seg + HIST_ROWS - (wl - 1) + k, :]
        ext_l[:, HIST_ROWS - (wl - 1):HIST_ROWS, :] = (
            ext_l[:, seg + HIST_ROWS - (wl - 1):seg + HIST_ROWS, :])

        c_bf = c.astype(jnp.bfloat16)
        neg_rate = -LRU_C * _softplus(p_ref[ap_row:ap_row + 1, :])
        for p in range(n_groups):
            cols = slice(p * V7X_MXU_DIM, (p + 1) * V7X_MXU_DIM)
            gates = jnp.dot(c_bf[:, cols], wg_ref[p], preferred_element_type=jnp.float32)
            r_gate = _sigmoid(gates[:, :V7X_MXU_DIM] + p_ref[ba_row:ba_row + 1, cols])
            i_gate = _sigmoid(gates[:, V7X_MXU_DIM:] + p_ref[bx_row:bx_row + 1, cols])
            a = jnp.exp(r_gate * neg_rate[:, cols])
            mult = jnp.sqrt(1.0 - a * a)
            ic = i_gate * c[:, cols]
            a_wr[:, cols] = a
            b_wr[:, cols] = mult * ic
            if reset_first:
                first = front_chunk == 0
                for s in range(n_seq):
                    r0 = s * seg
                    b_wr[r0:r0 + 1, cols] = jnp.where(
                        first, ic[r0:r0 + 1, :], (mult * ic)[r0:r0 + 1, :])
        oa_wr[...] = out_a.astype(jnp.bfloat16)
        gl_wr[...] = _gelu_tanh(g_l)

    def back_scan():
        assert seg % SCAN_BLOCK == 0
        h_in = None
        for r in range(0, rows, SCAN_BLOCK):
            s, t = divmod(r, seg)
            if t == 0:
                h_in = h_c[s:s + 1, :]
            a0, a1, a2, a3 = (a_rd[r + k:r + k + 1, :] for k in range(SCAN_BLOCK))
            b0, b1, b2, b3 = (b_rd[r + k:r + k + 1, :] for k in range(SCAN_BLOCK))
            a01 = a1 * a0
            b01 = a1 * b0 + b1
            a23 = a3 * a2
            b23 = a3 * b2 + b3
            h0 = a0 * h_in + b0
            h1 = a01 * h_in + b01
            h2 = a2 * h1 + b2
            h3 = (a23 * a01) * h_in + (a23 * b01 + b23)
            for k, h in enumerate((h0, h1, h2, h3)):
                h_s[r + k:r + k + 1, :] = h
            h_in = h3
            if t + SCAN_BLOCK == seg:
                h_c[s:s + 1, :] = h3
        nh_ref[pl.ds(back_seq0, n_seq), :] = h_c[...]

    def back_project():
        out_b = h_s[...] * gl_rd[...]
        mix = jnp.concatenate([oa_rd[...], out_b.astype(jnp.bfloat16)], axis=-1)
        x_prev = xp_ref[...].reshape(rows, d_model)
        x1 = x_prev + jnp.dot(mix, w_out_ref[...], preferred_element_type=jnp.float32)
        return x1, _rms_norm(x1, g2_ref[...]).astype(jnp.bfloat16)

    def back_mlp(acc, hm, f_lo, f_hi):
        for f in range(f_lo, f_hi):
            cols = slice(f * FF_CHUNK, (f + 1) * FF_CHUNK)
            up = jnp.dot(hm, w_up_ref[:, cols], preferred_element_type=jnp.float32)
            act = jnp.square(jnp.maximum(up, 0.0)).astype(jnp.bfloat16)
            acc = acc + jnp.dot(act, w_down_ref[cols, :], preferred_element_type=jnp.float32)
        return acc

    if skewed:
        proj = front_project()
        back_scan()
        x1, hm = back_project()
        x2 = back_mlp(x1, hm, 0, n_ff // 2)
        front_mix(proj)
        x2 = back_mlp(x2, hm, n_ff // 2, n_ff)
    else:
        front_mix(front_project())
        back_scan()
        x1, hm = back_project()
        x2 = back_mlp(x1, hm, 0, n_ff)
    y_ref[...] = _rms_norm(x2, gf_ref[...]).reshape(n_seq, seg, d_model)


def _resident(shape):
    zeros = (0,) * len(shape)
    return pl.BlockSpec(shape, lambda g: zeros, pipeline_mode=pl.Buffered(1))


def _run_layer(x, states, weights, *, seq_per_block, chunk, conv_widths, reset_first,
               skewed, name):
    n_seq, t_len, d_model = x.shape
    wa, wl = conv_widths
    chans = weights[2].shape[-1]
    assert n_seq % seq_per_block == 0 and t_len % chunk == 0
    assert chunk % V7X_SUBLANES == 0
    rows = seq_per_block * chunk
    n_chunks = t_len // chunk
    n_tiles = (n_seq // seq_per_block) * n_chunks
    blocked_state = seq_per_block % V7X_SUBLANES == 0
    zero_state = states is None
    assert zero_state or blocked_state

    def front_tile(g):
        return jnp.minimum(g, n_tiles - 1)

    def back_tile(g):
        return jnp.maximum(g - 1, 0) if skewed else g

    def tile_spec(tile_of):
        return pl.BlockSpec(
            (seq_per_block, chunk, d_model),
            lambda g: (tile_of(g) // n_chunks, tile_of(g) % n_chunks, 0))

    def rows_spec(n_rows, tile_of):
        if blocked_state:
            return pl.BlockSpec((n_rows, seq_per_block, chans),
                                lambda g: (0, tile_of(g) // n_chunks, 0))
        return pl.BlockSpec((n_rows, n_seq, chans), lambda g: (0, 0, 0))

    def h_spec(tile_of):
        if blocked_state:
            return pl.BlockSpec((seq_per_block, chans), lambda g: (tile_of(g) // n_chunks, 0))
        return pl.BlockSpec((n_seq, chans), lambda g: (0, 0))

    operands = [x, x]
    in_specs = [tile_spec(front_tile), tile_spec(back_tile)]
    if not zero_state:
        operands += list(states)
        in_specs += [rows_spec(wa - 1, front_tile), rows_spec(wl - 1, front_tile),
                     h_spec(back_tile)]
    operands += list(weights)
    in_specs += [_resident(w.shape) for w in weights]
    f32 = jnp.float32
    out_shape = (
        jax.ShapeDtypeStruct(x.shape, f32),
        jax.ShapeDtypeStruct((wa - 1, n_seq, chans), f32),
        jax.ShapeDtypeStruct((wl - 1, n_seq, chans), f32),
        jax.ShapeDtypeStruct((n_seq, chans), f32),
    )
    out_specs = (tile_spec(back_tile), rows_spec(wa - 1, front_tile),
                 rows_spec(wl - 1, front_tile), h_spec(back_tile))
    n_slots = 2 if skewed else 1
    scratch = [
        pltpu.VMEM((seq_per_block, HIST_ROWS + chunk, chans), f32),
        pltpu.VMEM((seq_per_block, HIST_ROWS + chunk, chans), f32),
        pltpu.VMEM((seq_per_block, chans), f32),
        pltpu.VMEM((n_slots, rows, chans), f32),
        pltpu.VMEM((n_slots, rows, chans), f32),
        pltpu.VMEM((rows, chans), f32),
        pltpu.VMEM((n_slots, rows, chans), jnp.bfloat16),
        pltpu.VMEM((n_slots, rows, chans), f32),
    ]
    return pl.pallas_call(
        functools.partial(
            _layer_kernel, n_chunks=n_chunks, n_tiles=n_tiles, conv_widths=conv_widths,
            reset_first=reset_first, skewed=skewed, zero_state=zero_state,
            blocked_state=blocked_state),
        grid=(n_tiles + 1 if skewed else n_tiles,),
        in_specs=in_specs,
        out_specs=out_specs,
        out_shape=out_shape,
        scratch_shapes=scratch,
        compiler_params=pltpu.CompilerParams(
            dimension_semantics=("arbitrary",),
            vmem_limit_bytes=V7X_VMEM_LIMIT_BYTES),
        name=name,
    )(*operands)


def _block_diag(w):
    n_heads, d, _ = w.shape
    eye = jnp.eye(n_heads, dtype=w.dtype)
    return (w[:, :, None, :] * eye[:, None, :, None]).reshape(n_heads * d, n_heads * d)


def _gate_weights(wa, wx):
    bd_a, bd_x = _block_diag(wa), _block_diag(wx)
    d_lru = bd_a.shape[0]
    groups = []
    for p in range(d_lru // V7X_MXU_DIM):
        sl = slice(p * V7X_MXU_DIM, (p + 1) * V7X_MXU_DIM)
        groups.append(jnp.concatenate([bd_a[sl, sl], bd_x[sl, sl]], axis=1))
    return jnp.stack(groups).astype(jnp.bfloat16)


def kernel(x_prompt, x_sample, state_conv_a, state_lru_conv, state_lru_h, norm1_g, w_in, conv_a_w, lru_conv_w, lru_conv_b, lru_wa, lru_ba, lru_wx, lru_bx, lru_a_param, w_out, norm2_g, w_up, w_down, norm_f_g):
    depth = w_in.shape[0]
    assert depth == 1
    dec_seq = x_sample.shape[1]
    d_conv = conv_a_w.shape[-1]
    d_lru = lru_conv_w.shape[-1]
    assert d_conv == d_lru
    conv_widths = (conv_a_w.shape[1], lru_conv_w.shape[1])
    bf16 = jnp.bfloat16
    l = 0
    small = jnp.concatenate([
        conv_a_w[l], lru_conv_w[l], lru_conv_b[l][None, :], lru_ba[l].reshape(1, d_lru),
        lru_bx[l].reshape(1, d_lru), lru_a_param[l][None, :]], axis=0)
    pad_rows = -small.shape[0] % V7X_SUBLANES
    small = jnp.pad(small, ((0, pad_rows), (0, 0)))
    weights = (
        norm1_g[l][None, :], w_in[l].astype(bf16), small,
        _gate_weights(lru_wa[l], lru_wx[l]), w_out[l].astype(bf16), norm2_g[l][None, :],
        w_up[l].astype(bf16), w_down[l].astype(bf16), norm_f_g[None, :],
    )

    def rows_first(a):
        return jnp.transpose(a, (1, 0, 2))

    yp, pa, pl_, ph = _run_layer(
        x_prompt, None, weights, seq_per_block=1, chunk=PROMPT_TILE_ROWS,
        conv_widths=conv_widths, reset_first=True, skewed=False, name="layer_prompt")
    sample_states = (rows_first(state_conv_a[l]), rows_first(state_lru_conv[l]),
                     state_lru_h[l])
    ys, sa, sl, sh = _run_layer(
        x_sample, sample_states, weights, seq_per_block=SAMPLE_TILE_ROWS // dec_seq,
        chunk=dec_seq, conv_widths=conv_widths, reset_first=False, skewed=False,
        name="layer_sample")
    return (yp, ys, rows_first(pa)[None], rows_first(pl_)[None], ph[None],
            rows_first(sa)[None], rows_first(sl)[None], sh[None])
```

```python
import functools

import jax
import jax.numpy as jnp
from jax import lax
from jax.experimental import pallas as pl
from jax.experimental.pallas import tpu as pltpu

LRU_C = 8.0
EPS = 1e-6
GELU_K0 = 0.7978845608028654
GELU_K1 = 0.044715

V7X_SUBLANES = 8
V7X_MXU_DIM = 256
V7X_VMEM_LIMIT_BYTES = 58 * 1024 * 1024
HIST_ROWS = V7X_SUBLANES
TILE_ROWS = 512
FF_CHUNK = 1024
SCAN_BLOCK = 4
N_SUB = 2


def _rms_norm(x, g):
    return x * lax.rsqrt(jnp.mean(x * x, axis=-1, keepdims=True) + EPS) * g


def _sigmoid(x):
    return 0.5 * jnp.tanh(0.5 * x) + 0.5


def _gelu_tanh(x):
    return 0.5 * x * (1.0 + jnp.tanh(GELU_K0 * (x + GELU_K1 * (x * x * x))))


def _softplus(x):
    return jnp.maximum(x, 0.0) + jnp.log1p(jnp.exp(-jnp.abs(x)))


def _causal_conv(ext, p_ref, w_row, width, seqs, t0, t_len):
    chans = ext.shape[-1]
    n_rows = (seqs.stop - seqs.start) * t_len
    acc = None
    for k in range(width):
        lo = HIST_ROWS + t0 - (width - 1) + k
        tap = p_ref[w_row + k:w_row + k + 1, :]
        term = ext[seqs, lo:lo + t_len, :].reshape(n_rows, chans) * tap
        acc = term if acc is None else acc + term
    return acc


def _layer_kernel(*refs, n_chunks, conv_widths, reset_first, zero_state, blocked_state):
    refs = list(refs)
    x_ref = refs.pop(0)
    if not zero_state:
        sa_ref, sl_ref, sh_ref = refs[:3]
        del refs[:3]
    (g1_ref, w_in_ref, p_ref, wg_ref, w_out_ref, g2_ref, w_up_ref, w_down_ref, gf_ref,
     y_ref, na_ref, nl_ref, nh_ref,
     ext_a, ext_l, h_c, a_s, b_s, h_s) = refs
    n_seq, seg, d_model = x_ref.shape
    rows = n_seq * seg
    d_conv = ext_a.shape[-1]
    d_lru = ext_l.shape[-1]
    wa, wl = conv_widths
    caw_row, lcw_row = 0, wa
    lcb_row, ba_row, bx_row, ap_row = (wa + wl + i for i in range(4))
    n_groups = d_lru // V7X_MXU_DIM
    g = pl.program_id(0)
    chunk = g % n_chunks
    seq0 = 0 if blocked_state else (g // n_chunks) * n_seq

    @pl.when(chunk == 0)
    def _():
        for k in range(wa - 1):
            row = jnp.zeros((n_seq, d_conv), jnp.float32) if zero_state else sa_ref[k]
            ext_a[:, HIST_ROWS - (wa - 1) + k, :] = row
        for k in range(wl - 1):
            row = jnp.zeros((n_seq, d_lru), jnp.float32) if zero_state else sl_ref[k]
            ext_l[:, HIST_ROWS - (wl - 1) + k, :] = row
        h_c[...] = jnp.zeros(h_c.shape, jnp.float32) if zero_state else sh_ref[...]

    sub_rows = rows // N_SUB
    lru_cols = slice(3 * d_conv, 3 * d_conv + 2 * d_lru)
    conv_cols = slice(0, 3 * d_conv)
    neg_rate = -LRU_C * _softplus(p_ref[ap_row:ap_row + 1, :])
    x = x_ref[...].reshape(rows, d_model)

    def sub_tile(sub):
        lo = sub * sub_rows
        if n_seq > 1:
            return lo, slice(sub * (n_seq // N_SUB), (sub + 1) * (n_seq // N_SUB)), 0, seg
        return lo, slice(0, 1), lo, sub_rows

    hns, lru_projs = [], []
    for sub in range(N_SUB):
        lo = sub * sub_rows
        hn = _rms_norm(x[lo:lo + sub_rows, :], g1_ref[...]).astype(jnp.bfloat16)
        hns.append(hn)
        lru_projs.append(
            jnp.dot(hn, w_in_ref[:, lru_cols], preferred_element_type=jnp.float32))

    h_in = None
    conv_projs = []
    for sub in range(N_SUB):
        lo, seqs, t0, t_len = sub_tile(sub)
        sub_seqs = seqs.stop - seqs.start
        x_l = lru_projs[sub][:, :d_lru]

        ext_l[seqs, HIST_ROWS + t0:HIST_ROWS + t0 + t_len, :] = (
            x_l.reshape(sub_seqs, t_len, d_lru))
        c = (_causal_conv(ext_l, p_ref, lcw_row, wl, seqs, t0, t_len)
             + p_ref[lcb_row:lcb_row + 1, :])
        c_bf = c.astype(jnp.bfloat16)
        gates = [jnp.dot(c_bf[:, p * V7X_MXU_DIM:(p + 1) * V7X_MXU_DIM], wg_ref[p],
                         preferred_element_type=jnp.float32) for p in range(n_groups)]
        conv_projs.append(
            jnp.dot(hns[sub], w_in_ref[:, conv_cols], preferred_element_type=jnp.float32))
        for p in range(n_groups):
            cols = slice(p * V7X_MXU_DIM, (p + 1) * V7X_MXU_DIM)
            r_gate = _sigmoid(gates[p][:, :V7X_MXU_DIM] + p_ref[ba_row:ba_row + 1, cols])
            i_gate = _sigmoid(gates[p][:, V7X_MXU_DIM:] + p_ref[bx_row:bx_row + 1, cols])
            a = jnp.exp(r_gate * neg_rate[:, cols])
            mult = jnp.sqrt(1.0 - a * a)
            ic = i_gate * c[:, cols]
            a_s[lo:lo + sub_rows, cols] = a
            b_s[lo:lo + sub_rows, cols] = mult * ic
            if reset_first and t0 == 0:
                for s in range(sub_seqs):
                    r0 = s * seg
                    b_s[lo + r0:lo + r0 + 1, cols] = jnp.where(
                        chunk == 0, ic[r0:r0 + 1, :], (mult * ic)[r0:r0 + 1, :])

        assert seg % SCAN_BLOCK == 0 and sub_rows % SCAN_BLOCK == 0
        for r in range(lo, lo + sub_rows, SCAN_BLOCK):
            s, t = divmod(r, seg)
            if t == 0:
                h_in = h_c[s:s + 1, :]
            a0, a1, a2, a3 = (a_s[r + k:r + k + 1, :] for k in range(SCAN_BLOCK))
            b0, b1, b2, b3 = (b_s[r + k:r + k + 1, :] for k in range(SCAN_BLOCK))
            a01 = a1 * a0
            b01 = a1 * b0 + b1
            a23 = a3 * a2
            b23 = a3 * b2 + b3
            h0 = a0 * h_in + b0
            h1 = a01 * h_in + b01
            h2 = a2 * h1 + b2
            h3 = (a23 * a01) * h_in + (a23 * b01 + b23)
            for k, h in enumerate((h0, h1, h2, h3)):
                h_s[r + k:r + k + 1, :] = h
            h_in = h3
            if t + SCAN_BLOCK == seg:
                h_c[s:s + 1, :] = h3

    x1_parts = []
    for sub in range(N_SUB):
        lo, seqs, t0, t_len = sub_tile(sub)
        sub_seqs = seqs.stop - seqs.start
        g_b = conv_projs[sub][:, 0:d_conv]
        g_c = conv_projs[sub][:, d_conv:2 * d_conv]
        x_a = conv_projs[sub][:, 2 * d_conv:3 * d_conv]
        g_l = lru_projs[sub][:, d_lru:]

        ext_a[seqs, HIST_ROWS + t0:HIST_ROWS + t0 + t_len, :] = (
            (g_c * x_a).reshape(sub_seqs, t_len, d_conv))
        out_a = g_b * _causal_conv(ext_a, p_ref, caw_row, wa, seqs, t0, t_len)

        out_b = h_s[lo:lo + sub_rows, :] * _gelu_tanh(g_l)
        mix = jnp.concatenate([out_a, out_b], axis=-1).astype(jnp.bfloat16)
        x1_parts.append(
            x[lo:lo + sub_rows, :]
            + jnp.dot(mix, w_out_ref[...], preferred_element_type=jnp.float32))
    x1 = jnp.concatenate(x1_parts, axis=0)

    for k in range(wa - 1):
        na_ref[k, pl.ds(seq0, n_seq), :] = ext_a[:, seg + HIST_ROWS - (wa - 1) + k, :]
    ext_a[:, HIST_ROWS - (wa - 1):HIST_ROWS, :] = (
        ext_a[:, seg + HIST_ROWS - (wa - 1):seg + HIST_ROWS, :])
    for k in range(wl - 1):
        nl_ref[k, pl.ds(seq0, n_seq), :] = ext_l[:, seg + HIST_ROWS - (wl - 1) + k, :]
    ext_l[:, HIST_ROWS - (wl - 1):HIST_ROWS, :] = (
        ext_l[:, seg + HIST_ROWS - (wl - 1):seg + HIST_ROWS, :])
    nh_ref[pl.ds(seq0, n_seq), :] = h_c[...]

    hm = _rms_norm(x1, g2_ref[...]).astype(jnp.bfloat16)
    x2 = x1
    for f in range(w_up_ref.shape[1] // FF_CHUNK):
        cols = slice(f * FF_CHUNK, (f + 1) * FF_CHUNK)
        up = jnp.dot(hm, w_up_ref[:, cols], preferred_element_type=jnp.float32)
        act = jnp.square(jnp.maximum(up, 0.0)).astype(jnp.bfloat16)
        x2 = x2 + jnp.dot(act, w_down_ref[cols, :], preferred_element_type=jnp.float32)
    y_ref[...] = _rms_norm(x2, gf_ref[...]).reshape(n_seq, seg, d_model)


def _resident(shape):
    zeros = (0,) * len(shape)
    return pl.BlockSpec(shape, lambda g: zeros, pipeline_mode=pl.Buffered(1))


def _run_layer(x, states, weights, *, seq_per_block, chunk, conv_widths, reset_first, name):
    n_seq, t_len, d_model = x.shape
    wa, wl = conv_widths
    chans = weights[2].shape[-1]
    assert n_seq % seq_per_block == 0 and t_len % chunk == 0
    assert chunk % V7X_SUBLANES == 0
    assert seq_per_block == 1 or seq_per_block % N_SUB == 0
    rows = seq_per_block * chunk
    n_chunks = t_len // chunk
    n_tiles = (n_seq // seq_per_block) * n_chunks
    blocked_state = seq_per_block % V7X_SUBLANES == 0
    zero_state = states is None
    assert zero_state or blocked_state

    def rows_spec(n_rows):
        if blocked_state:
            return pl.BlockSpec((n_rows, seq_per_block, chans), lambda g: (0, g // n_chunks, 0))
        return pl.BlockSpec((n_rows, n_seq, chans), lambda g: (0, 0, 0))

    if blocked_state:
        h_spec = pl.BlockSpec((seq_per_block, chans), lambda g: (g // n_chunks, 0))
    else:
        h_spec = pl.BlockSpec((n_seq, chans), lambda g: (0, 0))
    x_spec = pl.BlockSpec((seq_per_block, chunk, d_model),
                          lambda g: (g // n_chunks, g % n_chunks, 0))

    operands = [x]
    in_specs = [x_spec]
    if not zero_state:
        operands += list(states)
        in_specs += [rows_spec(wa - 1), rows_spec(wl - 1), h_spec]
    operands += list(weights)
    in_specs += [_resident(w.shape) for w in weights]
    f32 = jnp.float32
    out_shape = (
        jax.ShapeDtypeStruct(x.shape, f32),
        jax.ShapeDtypeStruct((wa - 1, n_seq, chans), f32),
        jax.ShapeDtypeStruct((wl - 1, n_seq, chans), f32),
        jax.ShapeDtypeStruct((n_seq, chans), f32),
    )
    out_specs = (x_spec, rows_spec(wa - 1), rows_spec(wl - 1), h_spec)
    scratch = [
        pltpu.VMEM((seq_per_block, HIST_ROWS + chunk, chans), f32),
        pltpu.VMEM((seq_per_block, HIST_ROWS + chunk, chans), f32),
        pltpu.VMEM((seq_per_block, chans), f32),
        pltpu.VMEM((rows, chans), f32),
        pltpu.VMEM((rows, chans), f32),
        pltpu.VMEM((rows, chans), f32),
    ]
    return pl.pallas_call(
        functools.partial(
            _layer_kernel, n_chunks=n_chunks, conv_widths=conv_widths,
            reset_first=reset_first, zero_state=zero_state, blocked_state=blocked_state),
        grid=(n_tiles,),
        in_specs=in_specs,
        out_specs=out_specs,
        out_shape=out_shape,
        scratch_shapes=scratch,
        compiler_params=pltpu.CompilerParams(
            dimension_semantics=("arbitrary",),
            vmem_limit_bytes=V7X_VMEM_LIMIT_BYTES),
        name=name,
    )(*operands)


def _block_diag(w):
    n_heads, d, _ = w.shape
    eye = jnp.eye(n_heads, dtype=w.dtype)
    return (w[:, :, None, :] * eye[:, None, :, None]).reshape(n_heads * d, n_heads * d)


def _gate_weights(wa, wx):
    bd_a, bd_x = _block_diag(wa), _block_diag(wx)
    d_lru = bd_a.shape[0]
    groups = []
    for p in range(d_lru // V7X_MXU_DIM):
        sl = slice(p * V7X_MXU_DIM, (p + 1) * V7X_MXU_DIM)
        groups.append(jnp.concatenate([bd_a[sl, sl], bd_x[sl, sl]], axis=1))
    return jnp.stack(groups).astype(jnp.bfloat16)


def kernel(x_prompt, x_sample, state_conv_a, state_lru_conv, state_lru_h, norm1_g, w_in, conv_a_w, lru_conv_w, lru_conv_b, lru_wa, lru_ba, lru_wx, lru_bx, lru_a_param, w_out, norm2_g, w_up, w_down, norm_f_g):
    depth = w_in.shape[0]
    assert depth == 1
    dec_seq = x_sample.shape[1]
    d_conv = conv_a_w.shape[-1]
    d_lru = lru_conv_w.shape[-1]
    assert d_conv == d_lru
    conv_widths = (conv_a_w.shape[1], lru_conv_w.shape[1])
    bf16 = jnp.bfloat16
    l = 0
    small = jnp.concatenate([
        conv_a_w[l], lru_conv_w[l], lru_conv_b[l][None, :], lru_ba[l].reshape(1, d_lru),
        lru_bx[l].reshape(1, d_lru), lru_a_param[l][None, :]], axis=0)
    pad_rows = -small.shape[0] % V7X_SUBLANES
    small = jnp.pad(small, ((0, pad_rows), (0, 0)))
    weights = (
        norm1_g[l][None, :], w_in[l].astype(bf16), small,
        _gate_weights(lru_wa[l], lru_wx[l]), w_out[l].astype(bf16), norm2_g[l][None, :],
        w_up[l].astype(bf16), w_down[l].astype(bf16), norm_f_g[None, :],
    )

    def rows_first(a):
        return jnp.transpose(a, (1, 0, 2))

    yp, pa, pl_, ph = _run_layer(
        x_prompt, None, weights, seq_per_block=1, chunk=TILE_ROWS,
        conv_widths=conv_widths, reset_first=True, name="layer_prompt")
    sample_states = (rows_first(state_conv_a[l]), rows_first(state_lru_conv[l]),
                     state_lru_h[l])
    ys, sa, sl, sh = _run_layer(
        x_sample, sample_states, weights, seq_per_block=TILE_ROWS // dec_seq,
        chunk=dec_seq, conv_widths=conv_widths, reset_first=False, name="layer_sample")
    return (yp, ys, rows_first(pa)[None], rows_first(pl_)[None], ph[None],
            rows_first(sa)[None], rows_first(sl)[None], sh[None])
```

```python
import functools

import jax
import jax.numpy as jnp
from jax import lax
from jax.experimental import pallas as pl
from jax.experimental.pallas import tpu as pltpu

LRU_C = 8.0
EPS = 1e-6
GELU_K0 = 0.7978845608028654
GELU_K1 = 0.044715

V7X_SUBLANES = 8
V7X_MXU_DIM = 256
V7X_VMEM_LIMIT_BYTES = 58 * 1024 * 1024
HIST_ROWS = V7X_SUBLANES
TILE_ROWS = 512
FF_CHUNK = 1024
SCAN_BLOCK = 4
N_SUB = 1


def _rms_norm(x, g):
    return x * lax.rsqrt(jnp.mean(x * x, axis=-1, keepdims=True) + EPS) * g


def _sigmoid(x):
    return 0.5 * jnp.tanh(0.5 * x) + 0.5


def _gelu_tanh(x):
    return 0.5 * x * (1.0 + jnp.tanh(GELU_K0 * (x + GELU_K1 * (x * x * x))))


def _softplus(x):
    return jnp.maximum(x, 0.0) + jnp.log1p(jnp.exp(-jnp.abs(x)))


def _causal_conv(ext, p_ref, w_row, width, seqs, t0, t_len):
    chans = ext.shape[-1]
    n_rows = (seqs.stop - seqs.start) * t_len
    acc = None
    for k in range(width):
        lo = HIST_ROWS + t0 - (width - 1) + k
        tap = p_ref[w_row + k:w_row + k + 1, :]
        term = ext[seqs, lo:lo + t_len, :].reshape(n_rows, chans) * tap
        acc = term if acc is None else acc + term
    return acc


def _layer_kernel(*refs, n_chunks, conv_widths, reset_first, zero_state, blocked_state):
    refs = list(refs)
    x_ref = refs.pop(0)
    if not zero_state:
        sa_ref, sl_ref, sh_ref = refs[:3]
        del refs[:3]
    (g1_ref, w_in_ref, p_ref, wg_ref, w_out_ref, g2_ref, w_up_ref, w_down_ref, gf_ref,
     y_ref, na_ref, nl_ref, nh_ref,
     ext_a, ext_l, h_c, a_s, b_s, h_s) = refs
    n_seq, seg, d_model = x_ref.shape
    rows = n_seq * seg
    d_conv = ext_a.shape[-1]
    d_lru = ext_l.shape[-1]
    wa, wl = conv_widths
    caw_row, lcw_row = 0, wa
    lcb_row, ba_row, bx_row, ap_row = (wa + wl + i for i in range(4))
    n_groups = d_lru // V7X_MXU_DIM
    g = pl.program_id(0)
    chunk = g % n_chunks
    seq0 = 0 if blocked_state else (g // n_chunks) * n_seq

    @pl.when(chunk == 0)
    def _():
        for k in range(wa - 1):
            row = jnp.zeros((n_seq, d_conv), jnp.float32) if zero_state else sa_ref[k]
            ext_a[:, HIST_ROWS - (wa - 1) + k, :] = row
        for k in range(wl - 1):
            row = jnp.zeros((n_seq, d_lru), jnp.float32) if zero_state else sl_ref[k]
            ext_l[:, HIST_ROWS - (wl - 1) + k, :] = row
        h_c[...] = jnp.zeros(h_c.shape, jnp.float32) if zero_state else sh_ref[...]

    sub_rows = rows // N_SUB
    lru_cols = slice(3 * d_conv, 3 * d_conv + 2 * d_lru)
    conv_cols = slice(0, 3 * d_conv)
    neg_rate = -LRU_C * _softplus(p_ref[ap_row:ap_row + 1, :])
    x = x_ref[...].reshape(rows, d_model)

    def sub_tile(sub):
        lo = sub * sub_rows
        if n_seq > 1:
            return lo, slice(sub * (n_seq // N_SUB), (sub + 1) * (n_seq // N_SUB)), 0, seg
        return lo, slice(0, 1), lo, sub_rows

    hns, lru_projs = [], []
    for sub in range(N_SUB):
        lo = sub * sub_rows
        hn = _rms_norm(x[lo:lo + sub_rows, :], g1_ref[...]).astype(jnp.bfloat16)
        hns.append(hn)
        lru_projs.append(
            jnp.dot(hn, w_in_ref[:, lru_cols], preferred_element_type=jnp.float32))

    h_in = None
    conv_projs = []
    for sub in range(N_SUB):
        lo, seqs, t0, t_len = sub_tile(sub)
        sub_seqs = seqs.stop - seqs.start
        x_l = lru_projs[sub][:, :d_lru]

        ext_l[seqs, HIST_ROWS + t0:HIST_ROWS + t0 + t_len, :] = (
            x_l.reshape(sub_seqs, t_len, d_lru))
        c = (_causal_conv(ext_l, p_ref, lcw_row, wl, seqs, t0, t_len)
             + p_ref[lcb_row:lcb_row + 1, :])
        c_bf = c.astype(jnp.bfloat16)
        gates = [jnp.dot(c_bf[:, p * V7X_MXU_DIM:(p + 1) * V7X_MXU_DIM], wg_ref[p],
                         preferred_element_type=jnp.float32) for p in range(n_groups)]
        conv_projs.append(
            jnp.dot(hns[sub], w_in_ref[:, conv_cols], preferred_element_type=jnp.float32))
        for p in range(n_groups):
            cols = slice(p * V7X_MXU_DIM, (p + 1) * V7X_MXU_DIM)
            r_gate = _sigmoid(gates[p][:, :V7X_MXU_DIM] + p_ref[ba_row:ba_row + 1, cols])
            i_gate = _sigmoid(gates[p][:, V7X_MXU_DIM:] + p_ref[bx_row:bx_row + 1, cols])
            a = jnp.exp(r_gate * neg_rate[:, cols])
            mult = jnp.sqrt(1.0 - a * a)
            ic = i_gate * c[:, cols]
            a_s[lo:lo + sub_rows, cols] = a
            b_s[lo:lo + sub_rows, cols] = mult * ic
            if reset_first and t0 == 0:
                for s in range(sub_seqs):
                    r0 = s * seg
                    b_s[lo + r0:lo + r0 + 1, cols] = jnp.where(
                        chunk == 0, ic[r0:r0 + 1, :], (mult * ic)[r0:r0 + 1, :])

        assert seg % SCAN_BLOCK == 0 and sub_rows % SCAN_BLOCK == 0
        for r in range(lo, lo + sub_rows, SCAN_BLOCK):
            s, t = divmod(r, seg)
            if t == 0:
                h_in = h_c[s:s + 1, :]
            a0, a1, a2, a3 = (a_s[r + k:r + k + 1, :] for k in range(SCAN_BLOCK))
            b0, b1, b2, b3 = (b_s[r + k:r + k + 1, :] for k in range(SCAN_BLOCK))
            a01 = a1 * a0
            b01 = a1 * b0 + b1
            a23 = a3 * a2
            b23 = a3 * b2 + b3
            h0 = a0 * h_in + b0
            h1 = a01 * h_in + b01
            h2 = a2 * h1 + b2
            h3 = (a23 * a01) * h_in + (a23 * b01 + b23)
            for k, h in enumerate((h0, h1, h2, h3)):
                h_s[r + k:r + k + 1, :] = h
            h_in = h3
            if t + SCAN_BLOCK == seg:
                h_c[s:s + 1, :] = h3

    x1_parts = []
    for sub in range(N_SUB):
        lo, seqs, t0, t_len = sub_tile(sub)
        sub_seqs = seqs.stop - seqs.start
        g_b = conv_projs[sub][:, 0:d_conv]
        g_c = conv_projs[sub][:, d_conv:2 * d_conv]
        x_a = conv_projs[sub][:, 2 * d_conv:3 * d_conv]
        g_l = lru_projs[sub][:, d_lru:]

        ext_a[seqs, HIST_ROWS + t0:HIST_ROWS + t0 + t_len, :] = (
            (g_c * x_a).reshape(sub_seqs, t_len, d_conv))
        out_a = g_b * _causal_conv(ext_a, p_ref, caw_row, wa, seqs, t0, t_len)

        out_b = h_s[lo:lo + sub_rows, :] * _gelu_tanh(g_l)
        mix = jnp.concatenate([out_a, out_b], axis=-1).astype(jnp.bfloat16)
        x1_parts.append(
            x[lo:lo + sub_rows, :]
            + jnp.dot(mix, w_out_ref[...], preferred_element_type=jnp.float32))
    x1 = jnp.concatenate(x1_parts, axis=0)

    for k in range(wa - 1):
        na_ref[k, pl.ds(seq0, n_seq), :] = ext_a[:, seg + HIST_ROWS - (wa - 1) + k, :]
    ext_a[:, HIST_ROWS - (wa - 1):HIST_ROWS, :] = (
        ext_a[:, seg + HIST_ROWS - (wa - 1):seg + HIST_ROWS, :])
    for k in range(wl - 1):
        nl_ref[k, pl.ds(seq0, n_seq), :] = ext_l[:, seg + HIST_ROWS - (wl - 1) + k, :]
    ext_l[:, HIST_ROWS - (wl - 1):HIST_ROWS, :] = (
        ext_l[:, seg + HIST_ROWS - (wl - 1):seg + HIST_ROWS, :])
    nh_ref[pl.ds(seq0, n_seq), :] = h_c[...]

    hm = _rms_norm(x1, g2_ref[...]).astype(jnp.bfloat16)
    x2 = x1
    for f in range(w_up_ref.shape[1] // FF_CHUNK):
        cols = slice(f * FF_CHUNK, (f + 1) * FF_CHUNK)
        up = jnp.dot(hm, w_up_ref[:, cols], preferred_element_type=jnp.float32)
        act = jnp.square(jnp.maximum(up, 0.0)).astype(jnp.bfloat16)
        x2 = x2 + jnp.dot(act, w_down_ref[cols, :], preferred_element_type=jnp.float32)
    y_ref[...] = _rms_norm(x2, gf_ref[...]).reshape(n_seq, seg, d_model)


def _resident(shape):
    zeros = (0,) * len(shape)
    return pl.BlockSpec(shape, lambda g: zeros, pipeline_mode=pl.Buffered(1))


def _run_layer(x, states, weights, *, seq_per_block, chunk, conv_widths, reset_first, name):
    n_seq, t_len, d_model = x.shape
    wa, wl = conv_widths
    chans = weights[2].shape[-1]
    assert n_seq % seq_per_block == 0 and t_len % chunk == 0
    assert chunk % V7X_SUBLANES == 0
    assert seq_per_block == 1 or seq_per_block % N_SUB == 0
    rows = seq_per_block * chunk
    n_chunks = t_len // chunk
    n_tiles = (n_seq // seq_per_block) * n_chunks
    blocked_state = seq_per_block % V7X_SUBLANES == 0
    zero_state = states is None
    assert zero_state or blocked_state

    def rows_spec(n_rows):
        if blocked_state:
            return pl.BlockSpec((n_rows, seq_per_block, chans), lambda g: (0, g // n_chunks, 0))
        return pl.BlockSpec((n_rows, n_seq, chans), lambda g: (0, 0, 0))

    if blocked_state:
        h_spec = pl.BlockSpec((seq_per_block, chans), lambda g: (g // n_chunks, 0))
    else:
        h_spec = pl.BlockSpec((n_seq, chans), lambda g: (0, 0))
    x_spec = pl.BlockSpec((seq_per_block, chunk, d_model),
                          lambda g: (g // n_chunks, g % n_chunks, 0))

    operands = [x]
    in_specs = [x_spec]
    if not zero_state:
        operands += list(states)
        in_specs += [rows_spec(wa - 1), rows_spec(wl - 1), h_spec]
    operands += list(weights)
    in_specs += [_resident(w.shape) for w in weights]
    f32 = jnp.float32
    out_shape = (
        jax.ShapeDtypeStruct(x.shape, f32),
        jax.ShapeDtypeStruct((wa - 1, n_seq, chans), f32),
        jax.ShapeDtypeStruct((wl - 1, n_seq, chans), f32),
        jax.ShapeDtypeStruct((n_seq, chans), f32),
    )
    out_specs = (x_spec, rows_spec(wa - 1), rows_spec(wl - 1), h_spec)
    scratch = [
        pltpu.VMEM((seq_per_block, HIST_ROWS + chunk, chans), f32),
        pltpu.VMEM((seq_per_block, HIST_ROWS + chunk, chans), f32),
        pltpu.VMEM((seq_per_block, chans), f32),
        pltpu.VMEM((rows, chans), f32),
        pltpu.VMEM((rows, chans), f32),
        pltpu.VMEM((rows, chans), f32),
    ]
    return pl.pallas_call(
        functools.partial(
            _layer_kernel, n_chunks=n_chunks, conv_widths=conv_widths,
            reset_first=reset_first, zero_state=zero_state, blocked_state=blocked_state),
        grid=(n_tiles,),
        in_specs=in_specs,
        out_specs=out_specs,
        out_shape=out_shape,
        scratch_shapes=scratch,
        compiler_params=pltpu.CompilerParams(
            dimension_semantics=("arbitrary",),
            vmem_limit_bytes=V7X_VMEM_LIMIT_BYTES),
        name=name,
    )(*operands)


def _block_diag(w):
    n_heads, d, _ = w.shape
    eye = jnp.eye(n_heads, dtype=w.dtype)
    return (w[:, :, None, :] * eye[:, None, :, None]).reshape(n_heads * d, n_heads * d)


def _gate_weights(wa, wx):
    bd_a, bd_x = _block_diag(wa), _block_diag(wx)
    d_lru = bd_a.shape[0]
    groups = []
    for p in range(d_lru // V7X_MXU_DIM):
        sl = slice(p * V7X_MXU_DIM, (p + 1) * V7X_MXU_DIM)
        groups.append(jnp.concatenate([bd_a[sl, sl], bd_x[sl, sl]], axis=1))
    return jnp.stack(groups).astype(jnp.bfloat16)


def kernel(x_prompt, x_sample, state_conv_a, state_lru_conv, state_lru_h, norm1_g, w_in, conv_a_w, lru_conv_w, lru_conv_b, lru_wa, lru_ba, lru_wx, lru_bx, lru_a_param, w_out, norm2_g, w_up, w_down, norm_f_g):
    depth = w_in.shape[0]
    assert depth == 1
    dec_seq = x_sample.shape[1]
    d_conv = conv_a_w.shape[-1]
    d_lru = lru_conv_w.shape[-1]
    assert d_conv == d_lru
    conv_widths = (conv_a_w.shape[1], lru_conv_w.shape[1])
    bf16 = jnp.bfloat16
    l = 0
    small = jnp.concatenate([
        conv_a_w[l], lru_conv_w[l], lru_conv_b[l][None, :], lru_ba[l].reshape(1, d_lru),
        lru_bx[l].reshape(1, d_lru), lru_a_param[l][None, :]], axis=0)
    pad_rows = -small.shape[0] % V7X_SUBLANES
    small = jnp.pad(small, ((0, pad_rows), (0, 0)))
    weights = (
        norm1_g[l][None, :], w_in[l].astype(bf16), small,
        _gate_weights(lru_wa[l], lru_wx[l]), w_out[l].astype(bf16), norm2_g[l][None, :],
        w_up[l].astype(bf16), w_down[l].astype(bf16), norm_f_g[None, :],
    )

    def rows_first(a):
        return jnp.transpose(a, (1, 0, 2))

    yp, pa, pl_, ph = _run_layer(
        x_prompt, None, weights, seq_per_block=1, chunk=TILE_ROWS,
        conv_widths=conv_widths, reset_first=True, name="layer_prompt")
    sample_states = (rows_first(state_conv_a[l]), rows_first(state_lru_conv[l]),
                     state_lru_h[l])
    ys, sa, sl, sh = _run_layer(
        x_sample, sample_states, weights, seq_per_block=TILE_ROWS // dec_seq,
        chunk=dec_seq, conv_widths=conv_widths, reset_first=False, name="layer_sample")
    return (yp, ys, rows_first(pa)[None], rows_first(pl_)[None], ph[None],
            rows_first(sa)[None], rows_first(sl)[None], sh[None])
```

```python
import functools

import jax
import jax.numpy as jnp
from jax import lax
from jax.experimental import pallas as pl
from jax.experimental.pallas import tpu as pltpu

LRU_C = 8.0
EPS = 1e-6
GELU_K0 = 0.7978845608028654
GELU_K1 = 0.044715

V7X_SUBLANES = 8
V7X_MXU_DIM = 256
V7X_VMEM_LIMIT_BYTES = 58 * 1024 * 1024
HIST_ROWS = V7X_SUBLANES
TILE_ROWS = 512
FF_CHUNK = 512
SCAN_BLOCK = 4
N_SUB = 1
MLP_TILES = 2


def _rms_norm(x, g):
    return x * lax.rsqrt(jnp.mean(x * x, axis=-1, keepdims=True) + EPS) * g


def _sigmoid(x):
    return 0.5 * jnp.tanh(0.5 * x) + 0.5


def _gelu_tanh(x):
    return 0.5 * x * (1.0 + jnp.tanh(GELU_K0 * (x + GELU_K1 * (x * x * x))))


def _softplus(x):
    return jnp.maximum(x, 0.0) + jnp.log1p(jnp.exp(-jnp.abs(x)))


def _causal_conv(ext, p_ref, w_row, width, seqs, t0, t_len):
    chans = ext.shape[-1]
    n_rows = (seqs.stop - seqs.start) * t_len
    acc = None
    for k in range(width):
        lo = HIST_ROWS + t0 - (width - 1) + k
        tap = p_ref[w_row + k:w_row + k + 1, :]
        term = ext[seqs, lo:lo + t_len, :].reshape(n_rows, chans) * tap
        acc = term if acc is None else acc + term
    return acc


def _layer_kernel(*refs, n_chunks, conv_widths, reset_first, zero_state, blocked_state):
    refs = list(refs)
    x_ref = refs.pop(0)
    if not zero_state:
        sa_ref, sl_ref, sh_ref = refs[:3]
        del refs[:3]
    (g1_ref, w_in_ref, p_ref, wg_ref, w_out_ref, g2_ref, w_up_ref, w_down_ref, gf_ref,
     y_ref, na_ref, nl_ref, nh_ref,
     ext_a, ext_l, h_c, a_s, b_s, h_s, x1_s, hm_s) = refs
    n_seq, seg, d_model = x_ref.shape
    rows = n_seq * seg
    d_conv = ext_a.shape[-1]
    d_lru = ext_l.shape[-1]
    wa, wl = conv_widths
    caw_row, lcw_row = 0, wa
    lcb_row, ba_row, bx_row, ap_row = (wa + wl + i for i in range(4))
    n_groups = d_lru // V7X_MXU_DIM
    g = pl.program_id(0)
    chunk = g % n_chunks
    seq0 = 0 if blocked_state else (g // n_chunks) * n_seq

    @pl.when(chunk == 0)
    def _():
        for k in range(wa - 1):
            row = jnp.zeros((n_seq, d_conv), jnp.float32) if zero_state else sa_ref[k]
            ext_a[:, HIST_ROWS - (wa - 1) + k, :] = row
        for k in range(wl - 1):
            row = jnp.zeros((n_seq, d_lru), jnp.float32) if zero_state else sl_ref[k]
            ext_l[:, HIST_ROWS - (wl - 1) + k, :] = row
        h_c[...] = jnp.zeros(h_c.shape, jnp.float32) if zero_state else sh_ref[...]

    sub_rows = rows // N_SUB
    lru_cols = slice(3 * d_conv, 3 * d_conv + 2 * d_lru)
    conv_cols = slice(0, 3 * d_conv)
    neg_rate = -LRU_C * _softplus(p_ref[ap_row:ap_row + 1, :])
    x = x_ref[...].reshape(rows, d_model)

    def sub_tile(sub):
        lo = sub * sub_rows
        if n_seq > 1:
            return lo, slice(sub * (n_seq // N_SUB), (sub + 1) * (n_seq // N_SUB)), 0, seg
        return lo, slice(0, 1), lo, sub_rows

    hns, lru_projs = [], []
    for sub in range(N_SUB):
        lo = sub * sub_rows
        hn = _rms_norm(x[lo:lo + sub_rows, :], g1_ref[...]).astype(jnp.bfloat16)
        hns.append(hn)
        lru_projs.append(
            jnp.dot(hn, w_in_ref[:, lru_cols], preferred_element_type=jnp.float32))

    h_in = None
    conv_projs = []
    for sub in range(N_SUB):
        lo, seqs, t0, t_len = sub_tile(sub)
        sub_seqs = seqs.stop - seqs.start
        x_l = lru_projs[sub][:, :d_lru]

        ext_l[seqs, HIST_ROWS + t0:HIST_ROWS + t0 + t_len, :] = (
            x_l.reshape(sub_seqs, t_len, d_lru))
        c = (_causal_conv(ext_l, p_ref, lcw_row, wl, seqs, t0, t_len)
             + p_ref[lcb_row:lcb_row + 1, :])
        c_bf = c.astype(jnp.bfloat16)
        gates = [jnp.dot(c_bf[:, p * V7X_MXU_DIM:(p + 1) * V7X_MXU_DIM], wg_ref[p],
                         preferred_element_type=jnp.float32) for p in range(n_groups)]
        conv_projs.append(
            jnp.dot(hns[sub], w_in_ref[:, conv_cols], preferred_element_type=jnp.float32))
        for p in range(n_groups):
            cols = slice(p * V7X_MXU_DIM, (p + 1) * V7X_MXU_DIM)
            r_gate = _sigmoid(gates[p][:, :V7X_MXU_DIM] + p_ref[ba_row:ba_row + 1, cols])
            i_gate = _sigmoid(gates[p][:, V7X_MXU_DIM:] + p_ref[bx_row:bx_row + 1, cols])
            a = jnp.exp(r_gate * neg_rate[:, cols])
            mult = jnp.sqrt(1.0 - a * a)
            ic = i_gate * c[:, cols]
            a_s[lo:lo + sub_rows, cols] = a
            b_s[lo:lo + sub_rows, cols] = mult * ic
            if reset_first and t0 == 0:
                for s in range(sub_seqs):
                    r0 = s * seg
                    b_s[lo + r0:lo + r0 + 1, cols] = jnp.where(
                        chunk == 0, ic[r0:r0 + 1, :], (mult * ic)[r0:r0 + 1, :])

        assert seg % SCAN_BLOCK == 0 and sub_rows % SCAN_BLOCK == 0
        for r in range(lo, lo + sub_rows, SCAN_BLOCK):
            s, t = divmod(r, seg)
            if t == 0:
                h_in = h_c[s:s + 1, :]
            a0, a1, a2, a3 = (a_s[r + k:r + k + 1, :] for k in range(SCAN_BLOCK))
            b0, b1, b2, b3 = (b_s[r + k:r + k + 1, :] for k in range(SCAN_BLOCK))
            a01 = a1 * a0
            b01 = a1 * b0 + b1
            a23 = a3 * a2
            b23 = a3 * b2 + b3
            h0 = a0 * h_in + b0
            h1 = a01 * h_in + b01
            h2 = a2 * h1 + b2
            h3 = (a23 * a01) * h_in + (a23 * b01 + b23)
            for k, h in enumerate((h0, h1, h2, h3)):
                h_s[r + k:r + k + 1, :] = h
            h_in = h3
            if t + SCAN_BLOCK == seg:
                h_c[s:s + 1, :] = h3

    x1_parts = []
    for sub in range(N_SUB):
        lo, seqs, t0, t_len = sub_tile(sub)
        sub_seqs = seqs.stop - seqs.start
        g_b = conv_projs[sub][:, 0:d_conv]
        g_c = conv_projs[sub][:, d_conv:2 * d_conv]
        x_a = conv_projs[sub][:, 2 * d_conv:3 * d_conv]
        g_l = lru_projs[sub][:, d_lru:]

        ext_a[seqs, HIST_ROWS + t0:HIST_ROWS + t0 + t_len, :] = (
            (g_c * x_a).reshape(sub_seqs, t_len, d_conv))
        out_a = g_b * _causal_conv(ext_a, p_ref, caw_row, wa, seqs, t0, t_len)

        out_b = h_s[lo:lo + sub_rows, :] * _gelu_tanh(g_l)
        mix = jnp.concatenate([out_a, out_b], axis=-1).astype(jnp.bfloat16)
        x1_parts.append(
            x[lo:lo + sub_rows, :]
            + jnp.dot(mix, w_out_ref[...], preferred_element_type=jnp.float32))
    x1 = jnp.concatenate(x1_parts, axis=0)

    for k in range(wa - 1):
        na_ref[k, pl.ds(seq0, n_seq), :] = ext_a[:, seg + HIST_ROWS - (wa - 1) + k, :]
    ext_a[:, HIST_ROWS - (wa - 1):HIST_ROWS, :] = (
        ext_a[:, seg + HIST_ROWS - (wa - 1):seg + HIST_ROWS, :])
    for k in range(wl - 1):
        nl_ref[k, pl.ds(seq0, n_seq), :] = ext_l[:, seg + HIST_ROWS - (wl - 1) + k, :]
    ext_l[:, HIST_ROWS - (wl - 1):HIST_ROWS, :] = (
        ext_l[:, seg + HIST_ROWS - (wl - 1):seg + HIST_ROWS, :])
    nh_ref[pl.ds(seq0, n_seq), :] = h_c[...]

    slot = g % MLP_TILES
    x1_s[slot] = x1
    hm_s[slot] = _rms_norm(x1, g2_ref[...]).astype(jnp.bfloat16)

    @pl.when(slot == MLP_TILES - 1)
    def _():
        hm = hm_s[...].reshape(MLP_TILES * rows, d_model)
        x2 = x1_s[...].reshape(MLP_TILES * rows, d_model)
        for f in range(w_up_ref.shape[1] // FF_CHUNK):
            cols = slice(f * FF_CHUNK, (f + 1) * FF_CHUNK)
            up = jnp.dot(hm, w_up_ref[:, cols], preferred_element_type=jnp.float32)
            act = jnp.square(jnp.maximum(up, 0.0)).astype(jnp.bfloat16)
            x2 = x2 + jnp.dot(act, w_down_ref[cols, :], preferred_element_type=jnp.float32)
        y_ref[...] = _rms_norm(x2, gf_ref[...]).reshape(y_ref.shape)


def _resident(shape):
    zeros = (0,) * len(shape)
    return pl.BlockSpec(shape, lambda g: zeros, pipeline_mode=pl.Buffered(1))


def _run_layer(x, states, weights, *, seq_per_block, chunk, conv_widths, reset_first, name):
    n_seq, t_len, d_model = x.shape
    wa, wl = conv_widths
    chans = weights[2].shape[-1]
    assert n_seq % seq_per_block == 0 and t_len % chunk == 0
    assert chunk % V7X_SUBLANES == 0
    assert seq_per_block == 1 or seq_per_block % N_SUB == 0
    rows = seq_per_block * chunk
    n_chunks = t_len // chunk
    n_tiles = (n_seq // seq_per_block) * n_chunks
    blocked_state = seq_per_block % V7X_SUBLANES == 0
    zero_state = states is None
    assert zero_state or blocked_state

    def rows_spec(n_rows):
        if blocked_state:
            return pl.BlockSpec((n_rows, seq_per_block, chans), lambda g: (0, g // n_chunks, 0))
        return pl.BlockSpec((n_rows, n_seq, chans), lambda g: (0, 0, 0))

    if blocked_state:
        h_spec = pl.BlockSpec((seq_per_block, chans), lambda g: (g // n_chunks, 0))
    else:
        h_spec = pl.BlockSpec((n_seq, chans), lambda g: (0, 0))
    x_spec = pl.BlockSpec((seq_per_block, chunk, d_model),
                          lambda g: (g // n_chunks, g % n_chunks, 0))

    if n_chunks > 1:
        assert n_chunks % MLP_TILES == 0
        y_spec = pl.BlockSpec(
            (seq_per_block, MLP_TILES * chunk, d_model),
            lambda g: (g // n_chunks, (g % n_chunks) // MLP_TILES, 0))
    else:
        assert n_tiles % MLP_TILES == 0
        y_spec = pl.BlockSpec((MLP_TILES * seq_per_block, chunk, d_model),
                              lambda g: (g // MLP_TILES, 0, 0))

    operands = [x]
    in_specs = [x_spec]
    if not zero_state:
        operands += list(states)
        in_specs += [rows_spec(wa - 1), rows_spec(wl - 1), h_spec]
    operands += list(weights)
    in_specs += [_resident(w.shape) for w in weights]
    f32 = jnp.float32
    out_shape = (
        jax.ShapeDtypeStruct(x.shape, f32),
        jax.ShapeDtypeStruct((wa - 1, n_seq, chans), f32),
        jax.ShapeDtypeStruct((wl - 1, n_seq, chans), f32),
        jax.ShapeDtypeStruct((n_seq, chans), f32),
    )
    out_specs = (y_spec, rows_spec(wa - 1), rows_spec(wl - 1), h_spec)
    scratch = [
        pltpu.VMEM((seq_per_block, HIST_ROWS + chunk, chans), f32),
        pltpu.VMEM((seq_per_block, HIST_ROWS + chunk, chans), f32),
        pltpu.VMEM((seq_per_block, chans), f32),
        pltpu.VMEM((rows, chans), f32),
        pltpu.VMEM((rows, chans), f32),
        pltpu.VMEM((rows, chans), f32),
        pltpu.VMEM((MLP_TILES, rows, d_model), f32),
        pltpu.VMEM((MLP_TILES, rows, d_model), jnp.bfloat16),
    ]
    return pl.pallas_call(
        functools.partial(
            _layer_kernel, n_chunks=n_chunks, conv_widths=conv_widths,
            reset_first=reset_first, zero_state=zero_state, blocked_state=blocked_state),
        grid=(n_tiles,),
        in_specs=in_specs,
        out_specs=out_specs,
        out_shape=out_shape,
        scratch_shapes=scratch,
        compiler_params=pltpu.CompilerParams(
            dimension_semantics=("arbitrary",),
            vmem_limit_bytes=V7X_VMEM_LIMIT_BYTES),
        name=name,
    )(*operands)


def _block_diag(w):
    n_heads, d, _ = w.shape
    eye = jnp.eye(n_heads, dtype=w.dtype)
    return (w[:, :, None, :] * eye[:, None, :, None]).reshape(n_heads * d, n_heads * d)


def _gate_weights(wa, wx):
    bd_a, bd_x = _block_diag(wa), _block_diag(wx)
    d_lru = bd_a.shape[0]
    groups = []
    for p in range(d_lru // V7X_MXU_DIM):
        sl = slice(p * V7X_MXU_DIM, (p + 1) * V7X_MXU_DIM)
        groups.append(jnp.concatenate([bd_a[sl, sl], bd_x[sl, sl]], axis=1))
    return jnp.stack(groups).astype(jnp.bfloat16)


def kernel(x_prompt, x_sample, state_conv_a, state_lru_conv, state_lru_h, norm1_g, w_in, conv_a_w, lru_conv_w, lru_conv_b, lru_wa, lru_ba, lru_wx, lru_bx, lru_a_param, w_out, norm2_g, w_up, w_down, norm_f_g):
    depth = w_in.shape[0]
    assert depth == 1
    dec_seq = x_sample.shape[1]
    d_conv = conv_a_w.shape[-1]
    d_lru = lru_conv_w.shape[-1]
    assert d_conv == d_lru
    conv_widths = (conv_a_w.shape[1], lru_conv_w.shape[1])
    bf16 = jnp.bfloat16
    l = 0
    small = jnp.concatenate([
        conv_a_w[l], lru_conv_w[l], lru_conv_b[l][None, :], lru_ba[l].reshape(1, d_lru),
        lru_bx[l].reshape(1, d_lru), lru_a_param[l][None, :]], axis=0)
    pad_rows = -small.shape[0] % V7X_SUBLANES
    small = jnp.pad(small, ((0, pad_rows), (0, 0)))
    weights = (
        norm1_g[l][None, :], w_in[l].astype(bf16), small,
        _gate_weights(lru_wa[l], lru_wx[l]), w_out[l].astype(bf16), norm2_g[l][None, :],
        w_up[l].astype(bf16), w_down[l].astype(bf16), norm_f_g[None, :],
    )

    def rows_first(a):
        return jnp.transpose(a, (1, 0, 2))

    yp, pa, pl_, ph = _run_layer(
        x_prompt, None, weights, seq_per_block=1, chunk=TILE_ROWS,
        conv_widths=conv_widths, reset_first=True, name="layer_prompt")
    sample_states = (rows_first(state_conv_a[l]), rows_first(state_lru_conv[l]),
                     state_lru_h[l])
    ys, sa, sl, sh = _run_layer(
        x_sample, sample_states, weights, seq_per_block=TILE_ROWS // dec_seq,
        chunk=dec_seq, conv_widths=conv_widths, reset_first=False, name="layer_sample")
    return (yp, ys, rows_first(pa)[None], rows_first(pl_)[None], ph[None],
            rows_first(sa)[None], rows_first(sl)[None], sh[None])
```

```python
import functools

import jax
import jax.numpy as jnp
from jax import lax
from jax.experimental import pallas as pl
from jax.experimental.pallas import tpu as pltpu

LRU_C = 8.0
EPS = 1e-6
GELU_K0 = 0.7978845608028654
GELU_K1 = 0.044715

V7X_SUBLANES = 8
V7X_MXU_DIM = 256
V7X_VMEM_LIMIT_BYTES = 58 * 1024 * 1024
HIST_ROWS = V7X_SUBLANES
TILE_ROWS = 512
FF_CHUNK = 1024
SCAN_BLOCK = 4
N_SUB = 1


def _rms_scale(x):
    return x * lax.rsqrt(jnp.mean(x * x, axis=-1, keepdims=True) + EPS)


def _sigmoid(x):
    return 0.5 * jnp.tanh(0.5 * x) + 0.5


def _gelu_tanh(x):
    return 0.5 * x * (1.0 + jnp.tanh(GELU_K0 * (x + GELU_K1 * (x * x * x))))


def _softplus(x):
    return jnp.maximum(x, 0.0) + jnp.log1p(jnp.exp(-jnp.abs(x)))


def _causal_conv(ext, p_ref, w_row, width, seqs, t0, t_len):
    chans = ext.shape[-1]
    n_rows = (seqs.stop - seqs.start) * t_len
    acc = None
    for k in range(width):
        lo = HIST_ROWS + t0 - (width - 1) + k
        tap = p_ref[w_row + k:w_row + k + 1, :]
        term = ext[seqs, lo:lo + t_len, :].reshape(n_rows, chans) * tap
        acc = term if acc is None else acc + term
    return acc


def _layer_kernel(*refs, n_chunks, conv_widths, reset_first, zero_state, blocked_state):
    refs = list(refs)
    x_ref = refs.pop(0)
    if not zero_state:
        sa_ref, sl_ref, sh_ref = refs[:3]
        del refs[:3]
    (w_in_ref, p_ref, wg_ref, w_out_ref, w_up_ref, w_down_ref, gf_ref,
     y_ref, na_ref, nl_ref, nh_ref,
     ext_a, ext_l, h_c, a_s, b_s, h_s) = refs
    n_seq, seg, d_model = x_ref.shape
    rows = n_seq * seg
    d_conv = ext_a.shape[-1]
    d_lru = ext_l.shape[-1]
    wa, wl = conv_widths
    caw_row, lcw_row = 0, wa
    lcb_row, ba_row, bx_row, ap_row = (wa + wl + i for i in range(4))
    n_groups = d_lru // V7X_MXU_DIM
    g = pl.program_id(0)
    chunk = g % n_chunks
    seq0 = 0 if blocked_state else (g // n_chunks) * n_seq

    @pl.when(chunk == 0)
    def _():
        for k in range(wa - 1):
            row = jnp.zeros((n_seq, d_conv), jnp.float32) if zero_state else sa_ref[k]
            ext_a[:, HIST_ROWS - (wa - 1) + k, :] = row
        for k in range(wl - 1):
            row = jnp.zeros((n_seq, d_lru), jnp.float32) if zero_state else sl_ref[k]
            ext_l[:, HIST_ROWS - (wl - 1) + k, :] = row
        h_c[...] = jnp.zeros(h_c.shape, jnp.float32) if zero_state else sh_ref[...]

    sub_rows = rows // N_SUB
    lru_cols = slice(3 * d_conv, 3 * d_conv + 2 * d_lru)
    conv_cols = slice(0, 3 * d_conv)
    neg_rate = -LRU_C * _softplus(p_ref[ap_row:ap_row + 1, :])
    x = x_ref[...].reshape(rows, d_model)

    def sub_tile(sub):
        lo = sub * sub_rows
        if n_seq > 1:
            return lo, slice(sub * (n_seq // N_SUB), (sub + 1) * (n_seq // N_SUB)), 0, seg
        return lo, slice(0, 1), lo, sub_rows

    hns, lru_projs = [], []
    for sub in range(N_SUB):
        lo = sub * sub_rows
        hn = _rms_scale(x[lo:lo + sub_rows, :]).astype(jnp.bfloat16)
        hns.append(hn)
        lru_projs.append(
            jnp.dot(hn, w_in_ref[:, lru_cols], preferred_element_type=jnp.float32))

    h_in = None
    conv_projs = []
    for sub in range(N_SUB):
        lo, seqs, t0, t_len = sub_tile(sub)
        sub_seqs = seqs.stop - seqs.start
        x_l = lru_projs[sub][:, :d_lru]

        ext_l[seqs, HIST_ROWS + t0:HIST_ROWS + t0 + t_len, :] = (
            x_l.reshape(sub_seqs, t_len, d_lru))
        c = (_causal_conv(ext_l, p_ref, lcw_row, wl, seqs, t0, t_len)
             + p_ref[lcb_row:lcb_row + 1, :])
        c_bf = c.astype(jnp.bfloat16)
        gates = [jnp.dot(c_bf[:, p * V7X_MXU_DIM:(p + 1) * V7X_MXU_DIM], wg_ref[p],
                         preferred_element_type=jnp.float32) for p in range(n_groups)]
        conv_projs.append(
            jnp.dot(hns[sub], w_in_ref[:, conv_cols], preferred_element_type=jnp.float32))
        for p in range(n_groups):
            cols = slice(p * V7X_MXU_DIM, (p + 1) * V7X_MXU_DIM)
            r_gate = _sigmoid(gates[p][:, :V7X_MXU_DIM] + p_ref[ba_row:ba_row + 1, cols])
            i_gate = _sigmoid(gates[p][:, V7X_MXU_DIM:] + p_ref[bx_row:bx_row + 1, cols])
            a = jnp.exp(r_gate * neg_rate[:, cols])
            one_m_a2 = 1.0 - a * a
            mult = jnp.where(one_m_a2 > 0.0, one_m_a2 * lax.rsqrt(one_m_a2), 0.0)
            ic = i_gate * c[:, cols]
            a_s[lo:lo + sub_rows, cols] = a
            b_s[lo:lo + sub_rows, cols] = mult * ic
            if reset_first and t0 == 0:
                for s in range(sub_seqs):
                    r0 = s * seg
                    b_s[lo + r0:lo + r0 + 1, cols] = jnp.where(
                        chunk == 0, ic[r0:r0 + 1, :], (mult * ic)[r0:r0 + 1, :])

        assert seg % SCAN_BLOCK == 0 and sub_rows % SCAN_BLOCK == 0
        for r in range(lo, lo + sub_rows, SCAN_BLOCK):
            s, t = divmod(r, seg)
            if t == 0:
                h_in = h_c[s:s + 1, :]
            a0, a1, a2, a3 = (a_s[r + k:r + k + 1, :] for k in range(SCAN_BLOCK))
            b0, b1, b2, b3 = (b_s[r + k:r + k + 1, :] for k in range(SCAN_BLOCK))
            a01 = a1 * a0
            b01 = a1 * b0 + b1
            a23 = a3 * a2
            b23 = a3 * b2 + b3
            h0 = a0 * h_in + b0
            h1 = a01 * h_in + b01
            h2 = a2 * h1 + b2
            h3 = (a23 * a01) * h_in + (a23 * b01 + b23)
            for k, h in enumerate((h0, h1, h2, h3)):
                h_s[r + k:r + k + 1, :] = h
            h_in = h3
            if t + SCAN_BLOCK == seg:
                h_c[s:s + 1, :] = h3

    x1_parts = []
    for sub in range(N_SUB):
        lo, seqs, t0, t_len = sub_tile(sub)
        sub_seqs = seqs.stop - seqs.start
        g_b = conv_projs[sub][:, 0:d_conv]
        g_c = conv_projs[sub][:, d_conv:2 * d_conv]
        x_a = conv_projs[sub][:, 2 * d_conv:3 * d_conv]
        g_l = lru_projs[sub][:, d_lru:]

        ext_a[seqs, HIST_ROWS + t0:HIST_ROWS + t0 + t_len, :] = (
            (g_c * x_a).reshape(sub_seqs, t_len, d_conv))
        out_a = g_b * _causal_conv(ext_a, p_ref, caw_row, wa, seqs, t0, t_len)

        out_b = h_s[lo:lo + sub_rows, :] * _gelu_tanh(g_l)
        mix = jnp.concatenate([out_a, out_b], axis=-1).astype(jnp.bfloat16)
        x1_parts.append(
            x[lo:lo + sub_rows, :]
            + jnp.dot(mix, w_out_ref[...], preferred_element_type=jnp.float32))
    x1 = jnp.concatenate(x1_parts, axis=0)

    for k in range(wa - 1):
        na_ref[k, pl.ds(seq0, n_seq), :] = ext_a[:, seg + HIST_ROWS - (wa - 1) + k, :]
    ext_a[:, HIST_ROWS - (wa - 1):HIST_ROWS, :] = (
        ext_a[:, seg + HIST_ROWS - (wa - 1):seg + HIST_ROWS, :])
    for k in range(wl - 1):
        nl_ref[k, pl.ds(seq0, n_seq), :] = ext_l[:, seg + HIST_ROWS - (wl - 1) + k, :]
    ext_l[:, HIST_ROWS - (wl - 1):HIST_ROWS, :] = (
        ext_l[:, seg + HIST_ROWS - (wl - 1):seg + HIST_ROWS, :])
    nh_ref[pl.ds(seq0, n_seq), :] = h_c[...]

    hm = _rms_scale(x1).astype(jnp.bfloat16)
    x2 = x1
    for f in range(w_up_ref.shape[1] // FF_CHUNK):
        cols = slice(f * FF_CHUNK, (f + 1) * FF_CHUNK)
        up = jnp.dot(hm, w_up_ref[:, cols], preferred_element_type=jnp.float32)
        act = jnp.square(jnp.maximum(up, 0.0)).astype(jnp.bfloat16)
        x2 = x2 + jnp.dot(act, w_down_ref[cols, :], preferred_element_type=jnp.float32)
    y_ref[...] = (_rms_scale(x2) * gf_ref[...]).reshape(n_seq, seg, d_model)


def _resident(shape):
    zeros = (0,) * len(shape)
    return pl.BlockSpec(shape, lambda g: zeros, pipeline_mode=pl.Buffered(1))


def _run_layer(x, states, weights, *, seq_per_block, chunk, conv_widths, reset_first, name):
    n_seq, t_len, d_model = x.shape
    wa, wl = conv_widths
    chans = weights[1].shape[-1]
    assert n_seq % seq_per_block == 0 and t_len % chunk == 0
    assert chunk % V7X_SUBLANES == 0
    assert seq_per_block == 1 or seq_per_block % N_SUB == 0
    rows = seq_per_block * chunk
    n_chunks = t_len // chunk
    n_tiles = (n_seq // seq_per_block) * n_chunks
    blocked_state = seq_per_block % V7X_SUBLANES == 0
    zero_state = states is None
    assert zero_state or blocked_state

    def rows_spec(n_rows):
        if blocked_state:
            return pl.BlockSpec((n_rows, seq_per_block, chans), lambda g: (0, g // n_chunks, 0))
        return pl.BlockSpec((n_rows, n_seq, chans), lambda g: (0, 0, 0))

    if blocked_state:
        h_spec = pl.BlockSpec((seq_per_block, chans), lambda g: (g // n_chunks, 0))
    else:
        h_spec = pl.BlockSpec((n_seq, chans), lambda g: (0, 0))
    x_spec = pl.BlockSpec((seq_per_block, chunk, d_model),
                          lambda g: (g // n_chunks, g % n_chunks, 0))

    operands = [x]
    in_specs = [x_spec]
    if not zero_state:
        operands += list(states)
        in_specs += [rows_spec(wa - 1), rows_spec(wl - 1), h_spec]
    operands += list(weights)
    in_specs += [_resident(w.shape) for w in weights]
    f32 = jnp.float32
    out_shape = (
        jax.ShapeDtypeStruct(x.shape, f32),
        jax.ShapeDtypeStruct((wa - 1, n_seq, chans), f32),
        jax.ShapeDtypeStruct((wl - 1, n_seq, chans), f32),
        jax.ShapeDtypeStruct((n_seq, chans), f32),
    )
    out_specs = (x_spec, rows_spec(wa - 1), rows_spec(wl - 1), h_spec)
    scratch = [
        pltpu.VMEM((seq_per_block, HIST_ROWS + chunk, chans), f32),
        pltpu.VMEM((seq_per_block, HIST_ROWS + chunk, chans), f32),
        pltpu.VMEM((seq_per_block, chans), f32),
        pltpu.VMEM((rows, chans), f32),
        pltpu.VMEM((rows, chans), f32),
        pltpu.VMEM((rows, chans), f32),
    ]
    return pl.pallas_call(
        functools.partial(
            _layer_kernel, n_chunks=n_chunks, conv_widths=conv_widths,
            reset_first=reset_first, zero_state=zero_state, blocked_state=blocked_state),
        grid=(n_tiles,),
        in_specs=in_specs,
        out_specs=out_specs,
        out_shape=out_shape,
        scratch_shapes=scratch,
        compiler_params=pltpu.CompilerParams(
            dimension_semantics=("arbitrary",),
            vmem_limit_bytes=V7X_VMEM_LIMIT_BYTES),
        name=name,
    )(*operands)


def _block_diag(w):
    n_heads, d, _ = w.shape
    eye = jnp.eye(n_heads, dtype=w.dtype)
    return (w[:, :, None, :] * eye[:, None, :, None]).reshape(n_heads * d, n_heads * d)


def _gate_weights(wa, wx):
    bd_a, bd_x = _block_diag(wa), _block_diag(wx)
    d_lru = bd_a.shape[0]
    groups = []
    for p in range(d_lru // V7X_MXU_DIM):
        sl = slice(p * V7X_MXU_DIM, (p + 1) * V7X_MXU_DIM)
        groups.append(jnp.concatenate([bd_a[sl, sl], bd_x[sl, sl]], axis=1))
    return jnp.stack(groups).astype(jnp.bfloat16)


def kernel(x_prompt, x_sample, state_conv_a, state_lru_conv, state_lru_h, norm1_g, w_in, conv_a_w, lru_conv_w, lru_conv_b, lru_wa, lru_ba, lru_wx, lru_bx, lru_a_param, w_out, norm2_g, w_up, w_down, norm_f_g):
    depth = w_in.shape[0]
    assert depth == 1
    dec_seq = x_sample.shape[1]
    d_conv = conv_a_w.shape[-1]
    d_lru = lru_conv_w.shape[-1]
    assert d_conv == d_lru
    conv_widths = (conv_a_w.shape[1], lru_conv_w.shape[1])
    bf16 = jnp.bfloat16
    l = 0
    small = jnp.concatenate([
        conv_a_w[l], lru_conv_w[l], lru_conv_b[l][None, :], lru_ba[l].reshape(1, d_lru),
        lru_bx[l].reshape(1, d_lru), lru_a_param[l][None, :]], axis=0)
    pad_rows = -small.shape[0] % V7X_SUBLANES
    small = jnp.pad(small, ((0, pad_rows), (0, 0)))
    weights = (
        (norm1_g[l][:, None] * w_in[l]).astype(bf16), small,
        _gate_weights(lru_wa[l], lru_wx[l]), w_out[l].astype(bf16),
        (norm2_g[l][:, None] * w_up[l]).astype(bf16), w_down[l].astype(bf16),
        norm_f_g[None, :],
    )

    def rows_first(a):
        return jnp.transpose(a, (1, 0, 2))

    yp, pa, pl_, ph = _run_layer(
        x_prompt, None, weights, seq_per_block=1, chunk=TILE_ROWS,
        conv_widths=conv_widths, reset_first=True, name="layer_prompt")
    sample_states = (rows_first(state_conv_a[l]), rows_first(state_lru_conv[l]),
                     state_lru_h[l])
    ys, sa, sl, sh = _run_layer(
        x_sample, sample_states, weights, seq_per_block=TILE_ROWS // dec_seq,
        chunk=dec_seq, conv_widths=conv_widths, reset_first=False, name="layer_sample")
    return (yp, ys, rows_first(pa)[None], rows_first(pl_)[None], ph[None],
            rows_first(sa)[None], rows_first(sl)[None], sh[None])
```

```python
import functools

import jax
import jax.numpy as jnp
from jax import lax
from jax.experimental import pallas as pl
from jax.experimental.pallas import tpu as pltpu

LRU_C = 8.0
EPS = 1e-6
GELU_K0 = 0.7978845608028654
GELU_K1 = 0.044715

V7X_SUBLANES = 8
V7X_MXU_DIM = 256
V7X_VMEM_LIMIT_BYTES = 58 * 1024 * 1024
HIST_ROWS = V7X_SUBLANES
TILE_ROWS = 512
SCAN_BLOCK = 4


def _rms_scale(x):
    return x * lax.rsqrt(jnp.mean(x * x, axis=-1, keepdims=True) + EPS)


def _sigmoid(x):
    return 0.5 * jnp.tanh(0.5 * x) + 0.5


def _gelu_tanh(x):
    return 0.5 * x * (1.0 + jnp.tanh(GELU_K0 * (x + GELU_K1 * (x * x * x))))


def _softplus(x):
    return jnp.maximum(x, 0.0) + jnp.log1p(jnp.exp(-jnp.abs(x)))


def _causal_conv(ext, p_ref, w_row, width, n_seq, seg):
    chans = ext.shape[-1]
    acc = None
    for k in range(width):
        lo = HIST_ROWS - (width - 1) + k
        tap = p_ref[w_row + k:w_row + k + 1, :]
        term = ext[:, lo:lo + seg, :].reshape(n_seq * seg, chans) * tap
        acc = term if acc is None else acc + term
    return acc


def _layer_kernel(*refs, n_chunks, conv_widths, reset_first, zero_state, blocked_state):
    refs = list(refs)
    x_ref = refs.pop(0)
    if not zero_state:
        sa_ref, sl_ref, sh_ref = refs[:3]
        del refs[:3]
    (w_in_ref, p_ref, wg_ref, w_out_ref, w_up_ref, w_down_ref, gf_ref,
     y_ref, na_ref, nl_ref, nh_ref,
     ext_a, ext_l, h_c, a_s, b_s, h_s) = refs
    n_seq, seg, d_model = x_ref.shape
    rows = n_seq * seg
    d_conv = ext_a.shape[-1]
    d_lru = ext_l.shape[-1]
    wa, wl = conv_widths
    caw_row, lcw_row = 0, wa
    lcb_row, ba_row, bx_row, ap_row = (wa + wl + i for i in range(4))
    n_groups = d_lru // V7X_MXU_DIM
    g = pl.program_id(0)
    chunk = g % n_chunks
    seq0 = 0 if blocked_state else (g // n_chunks) * n_seq

    @pl.when(chunk == 0)
    def _():
        for k in range(wa - 1):
            row = jnp.zeros((n_seq, d_conv), jnp.float32) if zero_state else sa_ref[k]
            ext_a[:, HIST_ROWS - (wa - 1) + k, :] = row
        for k in range(wl - 1):
            row = jnp.zeros((n_seq, d_lru), jnp.float32) if zero_state else sl_ref[k]
            ext_l[:, HIST_ROWS - (wl - 1) + k, :] = row
        h_c[...] = jnp.zeros(h_c.shape, jnp.float32) if zero_state else sh_ref[...]

    x = x_ref[...].reshape(rows, d_model)
    hn = _rms_scale(x).astype(jnp.bfloat16)
    lru_proj = jnp.dot(hn, w_in_ref[:, 3 * d_conv:], preferred_element_type=jnp.float32)
    x_l = lru_proj[:, :d_lru]
    g_l = lru_proj[:, d_lru:]

    ext_l[:, HIST_ROWS:, :] = x_l.reshape(n_seq, seg, d_lru)
    c = _causal_conv(ext_l, p_ref, lcw_row, wl, n_seq, seg) + p_ref[lcb_row:lcb_row + 1, :]
    c_bf = c.astype(jnp.bfloat16)
    gcxa_proj = jnp.dot(hn, w_in_ref[:, d_conv:3 * d_conv], preferred_element_type=jnp.float32)
    gates = [jnp.dot(c_bf[:, p * V7X_MXU_DIM:(p + 1) * V7X_MXU_DIM], wg_ref[p],
                     preferred_element_type=jnp.float32) for p in range(n_groups)]
    g_b = jnp.dot(hn, w_in_ref[:, :d_conv], preferred_element_type=jnp.float32)
    neg_rate = -LRU_C * _softplus(p_ref[ap_row:ap_row + 1, :])
    for p in range(n_groups):
        cols = slice(p * V7X_MXU_DIM, (p + 1) * V7X_MXU_DIM)
        r_gate = _sigmoid(gates[p][:, :V7X_MXU_DIM] + p_ref[ba_row:ba_row + 1, cols])
        i_gate = _sigmoid(gates[p][:, V7X_MXU_DIM:] + p_ref[bx_row:bx_row + 1, cols])
        a = jnp.exp(r_gate * neg_rate[:, cols])
        one_m_a2 = 1.0 - a * a
        mult = jnp.where(one_m_a2 > 0.0, one_m_a2 * lax.rsqrt(one_m_a2), 0.0)
        ic = i_gate * c[:, cols]
        a_s[:, cols] = a
        b_s[:, cols] = mult * ic
        if reset_first:
            for s in range(n_seq):
                r0 = s * seg
                b_s[r0:r0 + 1, cols] = jnp.where(
                    chunk == 0, ic[r0:r0 + 1, :], (mult * ic)[r0:r0 + 1, :])

    assert seg % SCAN_BLOCK == 0
    h_in = None
    for r in range(0, rows, SCAN_BLOCK):
        s, t = divmod(r, seg)
        if t == 0:
            h_in = h_c[s:s + 1, :]
        a0, a1, a2, a3 = (a_s[r + k:r + k + 1, :] for k in range(SCAN_BLOCK))
        b0, b1, b2, b3 = (b_s[r + k:r + k + 1, :] for k in range(SCAN_BLOCK))
        a01 = a1 * a0
        b01 = a1 * b0 + b1
        a23 = a3 * a2
        b23 = a3 * b2 + b3
        h0 = a0 * h_in + b0
        h1 = a01 * h_in + b01
        h2 = a2 * h1 + b2
        h3 = (a23 * a01) * h_in + (a23 * b01 + b23)
        for k, h in enumerate((h0, h1, h2, h3)):
            h_s[r + k:r + k + 1, :] = h
        h_in = h3
        if t + SCAN_BLOCK == seg:
            h_c[s:s + 1, :] = h3

    g_c = gcxa_proj[:, :d_conv]
    x_a = gcxa_proj[:, d_conv:]
    ext_a[:, HIST_ROWS:, :] = (g_c * x_a).reshape(n_seq, seg, d_conv)
    out_a = g_b * _causal_conv(ext_a, p_ref, caw_row, wa, n_seq, seg)

    for k in range(wa - 1):
        na_ref[k, pl.ds(seq0, n_seq), :] = ext_a[:, seg + HIST_ROWS - (wa - 1) + k, :]
    ext_a[:, HIST_ROWS - (wa - 1):HIST_ROWS, :] = (
        ext_a[:, seg + HIST_ROWS - (wa - 1):seg + HIST_ROWS, :])
    for k in range(wl - 1):
        nl_ref[k, pl.ds(seq0, n_seq), :] = ext_l[:, seg + HIST_ROWS - (wl - 1) + k, :]
    ext_l[:, HIST_ROWS - (wl - 1):HIST_ROWS, :] = (
        ext_l[:, seg + HIST_ROWS - (wl - 1):seg + HIST_ROWS, :])
    nh_ref[pl.ds(seq0, n_seq), :] = h_c[...]

    out_b = h_s[...] * _gelu_tanh(g_l)
    mix = jnp.concatenate([out_a, out_b], axis=-1).astype(jnp.bfloat16)
    x1 = x + jnp.dot(mix, w_out_ref[...], preferred_element_type=jnp.float32)
    hm = _rms_scale(x1).astype(jnp.bfloat16)
    up = jnp.dot(hm, w_up_ref[...], preferred_element_type=jnp.float32)
    act = jnp.square(jnp.maximum(up, 0.0)).astype(jnp.bfloat16)
    x2 = x1 + jnp.dot(act, w_down_ref[...], preferred_element_type=jnp.float32)
    y_ref[...] = (_rms_scale(x2) * gf_ref[...]).reshape(n_seq, seg, d_model)


def _resident(shape):
    zeros = (0,) * len(shape)
    return pl.BlockSpec(shape, lambda g: zeros, pipeline_mode=pl.Buffered(1))


def _run_layer(x, states, weights, *, seq_per_block, chunk, conv_widths, reset_first, name):
    n_seq, t_len, d_model = x.shape
    wa, wl = conv_widths
    chans = weights[1].shape[-1]
    assert n_seq % seq_per_block == 0 and t_len % chunk == 0
    assert chunk % V7X_SUBLANES == 0
    rows = seq_per_block * chunk
    n_chunks = t_len // chunk
    n_tiles = (n_seq // seq_per_block) * n_chunks
    blocked_state = seq_per_block % V7X_SUBLANES == 0
    zero_state = states is None
    assert zero_state or blocked_state

    def rows_spec(n_rows):
        if blocked_state:
            return pl.BlockSpec((n_rows, seq_per_block, chans), lambda g: (0, g // n_chunks, 0))
        return pl.BlockSpec((n_rows, n_seq, chans), lambda g: (0, 0, 0))

    if blocked_state:
        h_spec = pl.BlockSpec((seq_per_block, chans), lambda g: (g // n_chunks, 0))
    else:
        h_spec = pl.BlockSpec((n_seq, chans), lambda g: (0, 0))
    x_spec = pl.BlockSpec((seq_per_block, chunk, d_model),
                          lambda g: (g // n_chunks, g % n_chunks, 0))

    operands = [x]
    in_specs = [x_spec]
    if not zero_state:
        operands += list(states)
        in_specs += [rows_spec(wa - 1), rows_spec(wl - 1), h_spec]
    operands += list(weights)
    in_specs += [_resident(w.shape) for w in weights]
    f32 = jnp.float32
    out_shape = (
        jax.ShapeDtypeStruct(x.shape, f32),
        jax.ShapeDtypeStruct((wa - 1, n_seq, chans), f32),
        jax.ShapeDtypeStruct((wl - 1, n_seq, chans), f32),
        jax.ShapeDtypeStruct((n_seq, chans), f32),
    )
    out_specs = (x_spec, rows_spec(wa - 1), rows_spec(wl - 1), h_spec)
    scratch = [
        pltpu.VMEM((seq_per_block, HIST_ROWS + chunk, chans), f32),
        pltpu.VMEM((seq_per_block, HIST_ROWS + chunk, chans), f32),
        pltpu.VMEM((seq_per_block, chans), f32),
        pltpu.VMEM((rows, chans), f32),
        pltpu.VMEM((rows, chans), f32),
        pltpu.VMEM((rows, chans), f32),
    ]
    return pl.pallas_call(
        functools.partial(
            _layer_kernel, n_chunks=n_chunks, conv_widths=conv_widths,
            reset_first=reset_first, zero_state=zero_state, blocked_state=blocked_state),
        grid=(n_tiles,),
        in_specs=in_specs,
        out_specs=out_specs,
        out_shape=out_shape,
        scratch_shapes=scratch,
        compiler_params=pltpu.CompilerParams(
            dimension_semantics=("arbitrary",),
            vmem_limit_bytes=V7X_VMEM_LIMIT_BYTES),
        name=name,
    )(*operands)


def _block_diag(w):
    n_heads, d, _ = w.shape
    eye = jnp.eye(n_heads, dtype=w.dtype)
    return (w[:, :, None, :] * eye[:, None, :, None]).reshape(n_heads * d, n_heads * d)


def _gate_weights(wa, wx):
    bd_a, bd_x = _block_diag(wa), _block_diag(wx)
    d_lru = bd_a.shape[0]
    groups = []
    for p in range(d_lru // V7X_MXU_DIM):
        sl = slice(p * V7X_MXU_DIM, (p + 1) * V7X_MXU_DIM)
        groups.append(jnp.concatenate([bd_a[sl, sl], bd_x[sl, sl]], axis=1))
    return jnp.stack(groups).astype(jnp.bfloat16)


def kernel(x_prompt, x_sample, state_conv_a, state_lru_conv, state_lru_h, norm1_g, w_in, conv_a_w, lru_conv_w, lru_conv_b, lru_wa, lru_ba, lru_wx, lru_bx, lru_a_param, w_out, norm2_g, w_up, w_down, norm_f_g):
    depth = w_in.shape[0]
    assert depth == 1
    dec_seq = x_sample.shape[1]
    d_conv = conv_a_w.shape[-1]
    d_lru = lru_conv_w.shape[-1]
    assert d_conv == d_lru
    conv_widths = (conv_a_w.shape[1], lru_conv_w.shape[1])
    bf16 = jnp.bfloat16
    l = 0
    small = jnp.concatenate([
        conv_a_w[l], lru_conv_w[l], lru_conv_b[l][None, :], lru_ba[l].reshape(1, d_lru),
        lru_bx[l].reshape(1, d_lru), lru_a_param[l][None, :]], axis=0)
    pad_rows = -small.shape[0] % V7X_SUBLANES
    small = jnp.pad(small, ((0, pad_rows), (0, 0)))
    weights = (
        (norm1_g[l][:, None] * w_in[l]).astype(bf16), small,
        _gate_weights(lru_wa[l], lru_wx[l]), w_out[l].astype(bf16),
        (norm2_g[l][:, None] * w_up[l]).astype(bf16), w_down[l].astype(bf16),
        norm_f_g[None, :],
    )

    def rows_first(a):
        return jnp.transpose(a, (1, 0, 2))

    yp, pa, pl_, ph = _run_layer(
        x_prompt, None, weights, seq_per_block=1, chunk=TILE_ROWS,
        conv_widths=conv_widths, reset_first=True, name="layer_prompt")
    sample_states = (rows_first(state_conv_a[l]), rows_first(state_lru_conv[l]),
                     state_lru_h[l])
    ys, sa, sl, sh = _run_layer(
        x_sample, sample_states, weights, seq_per_block=TILE_ROWS // dec_seq,
        chunk=dec_seq, conv_widths=conv_widths, reset_first=False, name="layer_sample")
    return (yp, ys, rows_first(pa)[None], rows_first(pl_)[None], ph[None],
            rows_first(sa)[None], rows_first(sl)[None], sh[None])
```

```python
import functools

import jax
import jax.numpy as jnp
from jax import lax
from jax.experimental import pallas as pl
from jax.experimental.pallas import tpu as pltpu

LRU_C = 8.0
EPS = 1e-6
GELU_K0 = 0.7978845608028654
GELU_K1 = 0.044715

V7X_SUBLANES = 8
V7X_MXU_DIM = 256
V7X_VMEM_LIMIT_BYTES = 58 * 1024 * 1024
HIST_ROWS = V7X_SUBLANES
TILE_ROWS = 512
SCAN_BLOCK = 4


def _rms_scale(x):
    return x * lax.rsqrt(jnp.mean(x * x, axis=-1, keepdims=True) + EPS)


def _sigmoid(x):
    return 0.5 * jnp.tanh(0.5 * x) + 0.5


def _gelu_tanh(x):
    return 0.5 * x * (1.0 + jnp.tanh(GELU_K0 * (x + GELU_K1 * (x * x * x))))


def _softplus(x):
    return jnp.maximum(x, 0.0) + jnp.log1p(jnp.exp(-jnp.abs(x)))


def _causal_conv(ext, p_ref, w_row, width, n_seq, seg):
    chans = ext.shape[-1]
    acc = None
    for k in range(width):
        lo = HIST_ROWS - (width - 1) + k
        tap = p_ref[w_row + k:w_row + k + 1, :]
        term = ext[:, lo:lo + seg, :].reshape(n_seq * seg, chans) * tap
        acc = term if acc is None else acc + term
    return acc


def _layer_kernel(*refs, n_chunks, conv_widths, reset_first, zero_state, blocked_state):
    refs = list(refs)
    x_ref = refs.pop(0)
    if not zero_state:
        sa_ref, sl_ref, sh_ref = refs[:3]
        del refs[:3]
    (w_in_ref, p_ref, wg_ref, w_out_ref, w_up_ref, w_down_ref, gf_ref,
     y_ref, na_ref, nl_ref, nh_ref,
     ext_a, ext_l, h_c, a_s, b_s, h_s) = refs
    n_seq, seg, d_model = x_ref.shape
    rows = n_seq * seg
    d_conv = ext_a.shape[-1]
    d_lru = ext_l.shape[-1]
    wa, wl = conv_widths
    caw_row, lcw_row = 0, wa
    lcb_row, ba_row, bx_row, ap_row = (wa + wl + i for i in range(4))
    n_groups = d_lru // V7X_MXU_DIM
    g = pl.program_id(0)
    chunk = g % n_chunks
    seq0 = 0 if blocked_state else (g // n_chunks) * n_seq

    @pl.when(chunk == 0)
    def _():
        for k in range(wa - 1):
            row = jnp.zeros((n_seq, d_conv), jnp.float32) if zero_state else sa_ref[k]
            ext_a[:, HIST_ROWS - (wa - 1) + k, :] = row
        for k in range(wl - 1):
            row = jnp.zeros((n_seq, d_lru), jnp.float32) if zero_state else sl_ref[k]
            ext_l[:, HIST_ROWS - (wl - 1) + k, :] = row
        h_c[...] = jnp.zeros(h_c.shape, jnp.float32) if zero_state else sh_ref[...]

    x = x_ref[...].reshape(rows, d_model)
    hn = _rms_scale(x).astype(jnp.bfloat16)
    x_l = jnp.dot(hn, w_in_ref[:, 3 * d_conv:3 * d_conv + d_lru],
                  preferred_element_type=jnp.float32)

    ext_l[:, HIST_ROWS:, :] = x_l.reshape(n_seq, seg, d_lru)
    c = _causal_conv(ext_l, p_ref, lcw_row, wl, n_seq, seg) + p_ref[lcb_row:lcb_row + 1, :]
    c_bf = c.astype(jnp.bfloat16)
    gcxa_proj = jnp.dot(hn, w_in_ref[:, d_conv:3 * d_conv], preferred_element_type=jnp.float32)
    gates = [jnp.dot(c_bf[:, p * V7X_MXU_DIM:(p + 1) * V7X_MXU_DIM], wg_ref[p],
                     preferred_element_type=jnp.float32) for p in range(n_groups)]
    g_l = jnp.dot(hn, w_in_ref[:, 3 * d_conv + d_lru:], preferred_element_type=jnp.float32)
    g_b = jnp.dot(hn, w_in_ref[:, :d_conv], preferred_element_type=jnp.float32)
    neg_rate = -LRU_C * _softplus(p_ref[ap_row:ap_row + 1, :])
    for p in range(n_groups):
        cols = slice(p * V7X_MXU_DIM, (p + 1) * V7X_MXU_DIM)
        r_gate = _sigmoid(gates[p][:, :V7X_MXU_DIM] + p_ref[ba_row:ba_row + 1, cols])
        i_gate = _sigmoid(gates[p][:, V7X_MXU_DIM:] + p_ref[bx_row:bx_row + 1, cols])
        a = jnp.exp(r_gate * neg_rate[:, cols])
        one_m_a2 = 1.0 - a * a
        mult = jnp.where(one_m_a2 > 0.0, one_m_a2 * lax.rsqrt(one_m_a2), 0.0)
        ic = i_gate * c[:, cols]
        a_s[:, cols] = a
        b_s[:, cols] = mult * ic
        if reset_first:
            for s in range(n_seq):
                r0 = s * seg
                b_s[r0:r0 + 1, cols] = jnp.where(
                    chunk == 0, ic[r0:r0 + 1, :], (mult * ic)[r0:r0 + 1, :])

    assert seg % SCAN_BLOCK == 0
    h_in = None
    for r in range(0, rows, SCAN_BLOCK):
        s, t = divmod(r, seg)
        if t == 0:
            h_in = h_c[s:s + 1, :]
        a0, a1, a2, a3 = (a_s[r + k:r + k + 1, :] for k in range(SCAN_BLOCK))
        b0, b1, b2, b3 = (b_s[r + k:r + k + 1, :] for k in range(SCAN_BLOCK))
        a01 = a1 * a0
        b01 = a1 * b0 + b1
        a23 = a3 * a2
        b23 = a3 * b2 + b3
        h0 = a0 * h_in + b0
        h1 = a01 * h_in + b01
        h2 = a2 * h1 + b2
        h3 = (a23 * a01) * h_in + (a23 * b01 + b23)
        for k, h in enumerate((h0, h1, h2, h3)):
            h_s[r + k:r + k + 1, :] = h
        h_in = h3
        if t + SCAN_BLOCK == seg:
            h_c[s:s + 1, :] = h3

    g_c = gcxa_proj[:, :d_conv]
    x_a = gcxa_proj[:, d_conv:]
    ext_a[:, HIST_ROWS:, :] = (g_c * x_a).reshape(n_seq, seg, d_conv)
    out_a = g_b * _causal_conv(ext_a, p_ref, caw_row, wa, n_seq, seg)

    for k in range(wa - 1):
        na_ref[k, pl.ds(seq0, n_seq), :] = ext_a[:, seg + HIST_ROWS - (wa - 1) + k, :]
    ext_a[:, HIST_ROWS - (wa - 1):HIST_ROWS, :] = (
        ext_a[:, seg + HIST_ROWS - (wa - 1):seg + HIST_ROWS, :])
    for k in range(wl - 1):
        nl_ref[k, pl.ds(seq0, n_seq), :] = ext_l[:, seg + HIST_ROWS - (wl - 1) + k, :]
    ext_l[:, HIST_ROWS - (wl - 1):HIST_ROWS, :] = (
        ext_l[:, seg + HIST_ROWS - (wl - 1):seg + HIST_ROWS, :])
    nh_ref[pl.ds(seq0, n_seq), :] = h_c[...]

    out_b = h_s[...] * _gelu_tanh(g_l)
    mix = jnp.concatenate([out_a, out_b], axis=-1).astype(jnp.bfloat16)
    x1 = x + jnp.dot(mix, w_out_ref[...], preferred_element_type=jnp.float32)
    hm = _rms_scale(x1).astype(jnp.bfloat16)
    up = jnp.dot(hm, w_up_ref[...], preferred_element_type=jnp.float32)
    act = jnp.square(jnp.maximum(up, 0.0)).astype(jnp.bfloat16)
    x2 = x1 + jnp.dot(act, w_down_ref[...], preferred_element_type=jnp.float32)
    y_ref[...] = (_rms_scale(x2) * gf_ref[...]).reshape(n_seq, seg, d_model)


def _resident(shape):
    zeros = (0,) * len(shape)
    return pl.BlockSpec(shape, lambda g: zeros, pipeline_mode=pl.Buffered(1))


def _run_layer(x, states, weights, *, seq_per_block, chunk, conv_widths, reset_first, name):
    n_seq, t_len, d_model = x.shape
    wa, wl = conv_widths
    chans = weights[1].shape[-1]
    assert n_seq % seq_per_block == 0 and t_len % chunk == 0
    assert chunk % V7X_SUBLANES == 0
    rows = seq_per_block * chunk
    n_chunks = t_len // chunk
    n_tiles = (n_seq // seq_per_block) * n_chunks
    blocked_state = seq_per_block % V7X_SUBLANES == 0
    zero_state = states is None
    assert zero_state or blocked_state

    def rows_spec(n_rows):
        if blocked_state:
            return pl.BlockSpec((n_rows, seq_per_block, chans), lambda g: (0, g // n_chunks, 0))
        return pl.BlockSpec((n_rows, n_seq, chans), lambda g: (0, 0, 0))

    if blocked_state:
        h_spec = pl.BlockSpec((seq_per_block, chans), lambda g: (g // n_chunks, 0))
    else:
        h_spec = pl.BlockSpec((n_seq, chans), lambda g: (0, 0))
    x_spec = pl.BlockSpec((seq_per_block, chunk, d_model),
                          lambda g: (g // n_chunks, g % n_chunks, 0))

    operands = [x]
    in_specs = [x_spec]
    if not zero_state:
        operands += list(states)
        in_specs += [rows_spec(wa - 1), rows_spec(wl - 1), h_spec]
    operands += list(weights)
    in_specs += [_resident(w.shape) for w in weights]
    f32 = jnp.float32
    out_shape = (
        jax.ShapeDtypeStruct(x.shape, f32),
        jax.ShapeDtypeStruct((wa - 1, n_seq, chans), f32),
        jax.ShapeDtypeStruct((wl - 1, n_seq, chans), f32),
        jax.ShapeDtypeStruct((n_seq, chans), f32),
    )
    out_specs = (x_spec, rows_spec(wa - 1), rows_spec(wl - 1), h_spec)
    scratch = [
        pltpu.VMEM((seq_per_block, HIST_ROWS + chunk, chans), f32),
        pltpu.VMEM((seq_per_block, HIST_ROWS + chunk, chans), f32),
        pltpu.VMEM((seq_per_block, chans), f32),
        pltpu.VMEM((rows, chans), f32),
        pltpu.VMEM((rows, chans), f32),
        pltpu.VMEM((rows, chans), f32),
    ]
    return pl.pallas_call(
        functools.partial(
            _layer_kernel, n_chunks=n_chunks, conv_widths=conv_widths,
            reset_first=reset_first, zero_state=zero_state, blocked_state=blocked_state),
        grid=(n_tiles,),
        in_specs=in_specs,
        out_specs=out_specs,
        out_shape=out_shape,
        scratch_shapes=scratch,
        compiler_params=pltpu.CompilerParams(
            dimension_semantics=("arbitrary",),
            vmem_limit_bytes=V7X_VMEM_LIMIT_BYTES),
        name=name,
    )(*operands)


def _block_diag(w):
    n_heads, d, _ = w.shape
    eye = jnp.eye(n_heads, dtype=w.dtype)
    return (w[:, :, None, :] * eye[:, None, :, None]).reshape(n_heads * d, n_heads * d)


def _gate_weights(wa, wx):
    bd_a, bd_x = _block_diag(wa), _block_diag(wx)
    d_lru = bd_a.shape[0]
    groups = []
    for p in range(d_lru // V7X_MXU_DIM):
        sl = slice(p * V7X_MXU_DIM, (p + 1) * V7X_MXU_DIM)
        groups.append(jnp.concatenate([bd_a[sl, sl], bd_x[sl, sl]], axis=1))
    return jnp.stack(groups).astype(jnp.bfloat16)


def kernel(x_prompt, x_sample, state_conv_a, state_lru_conv, state_lru_h, norm1_g, w_in, conv_a_w, lru_conv_w, lru_conv_b, lru_wa, lru_ba, lru_wx, lru_bx, lru_a_param, w_out, norm2_g, w_up, w_down, norm_f_g):
    depth = w_in.shape[0]
    assert depth == 1
    dec_seq = x_sample.shape[1]
    d_conv = conv_a_w.shape[-1]
    d_lru = lru_conv_w.shape[-1]
    assert d_conv == d_lru
    conv_widths = (conv_a_w.shape[1], lru_conv_w.shape[1])
    bf16 = jnp.bfloat16
    l = 0
    small = jnp.concatenate([
        conv_a_w[l], lru_conv_w[l], lru_conv_b[l][None, :], lru_ba[l].reshape(1, d_lru),
        lru_bx[l].reshape(1, d_lru), lru_a_param[l][None, :]], axis=0)
    pad_rows = -small.shape[0] % V7X_SUBLANES
    small = jnp.pad(small, ((0, pad_rows), (0, 0)))
    weights = (
        (norm1_g[l][:, None] * w_in[l]).astype(bf16), small,
        _gate_weights(lru_wa[l], lru_wx[l]), w_out[l].astype(bf16),
        (norm2_g[l][:, None] * w_up[l]).astype(bf16), w_down[l].astype(bf16),
        norm_f_g[None, :],
    )

    def rows_first(a):
        return jnp.transpose(a, (1, 0, 2))

    yp, pa, pl_, ph = _run_layer(
        x_prompt, None, weights, seq_per_block=1, chunk=TILE_ROWS,
        conv_widths=conv_widths, reset_first=True, name="layer_prompt")
    sample_states = (rows_first(state_conv_a[l]), rows_first(state_lru_conv[l]),
                     state_lru_h[l])
    ys, sa, sl, sh = _run_layer(
        x_sample, sample_states, weights, seq_per_block=TILE_ROWS // dec_seq,
        chunk=dec_seq, conv_widths=conv_widths, reset_first=False, name="layer_sample")
    return (yp, ys, rows_first(pa)[None], rows_first(pl_)[None], ph[None],
            rows_first(sa)[None], rows_first(sl)[None], sh[None])
```

```python
import functools

import jax
import jax.numpy as jnp
from jax import lax
from jax.experimental import pallas as pl
from jax.experimental.pallas import tpu as pltpu

LRU_C = 8.0
EPS = 1e-6
GELU_K0 = 0.7978845608028654
GELU_K1 = 0.044715

V7X_SUBLANES = 8
V7X_MXU_DIM = 256
V7X_VMEM_LIMIT_BYTES = 58 * 1024 * 1024
HIST_ROWS = V7X_SUBLANES
TILE_ROWS = 512
SCAN_BLOCK = 4


def _rms_scale(x):
    return x * lax.rsqrt(jnp.mean(x * x, axis=-1, keepdims=True) + EPS)


def _sigmoid(x):
    return 0.5 * jnp.tanh(0.5 * x) + 0.5


def _gelu_tanh(x):
    return 0.5 * x * (1.0 + jnp.tanh(GELU_K0 * (x + GELU_K1 * (x * x * x))))


def _softplus(x):
    return jnp.maximum(x, 0.0) + jnp.log1p(jnp.exp(-jnp.abs(x)))


def _causal_conv(ext, p_ref, w_row, width, n_seq, seg):
    chans = ext.shape[-1]
    acc = None
    for k in reversed(range(width)):
        lo = HIST_ROWS - (width - 1) + k
        tap = p_ref[w_row + k:w_row + k + 1, :]
        term = ext[:, lo:lo + seg, :].reshape(n_seq * seg, chans) * tap
        acc = term if acc is None else acc + term
    return acc


def _layer_kernel(*refs, n_chunks, conv_widths, reset_first, zero_state, blocked_state):
    refs = list(refs)
    x_ref = refs.pop(0)
    if not zero_state:
        sa_ref, sl_ref, sh_ref = refs[:3]
        del refs[:3]
    (w_in_ref, p_ref, wg_ref, w_out_ref, w_up_ref, w_down_ref, gf_ref,
     y_ref, na_ref, nl_ref, nh_ref,
     ext_a, ext_l, h_c, a_s, b_s, h_s) = refs
    n_seq, seg, d_model = x_ref.shape
    rows = n_seq * seg
    d_conv = ext_a.shape[-1]
    d_lru = ext_l.shape[-1]
    wa, wl = conv_widths
    caw_row, lcw_row = 0, wa
    lcb_row, ba_row, bx_row, ap_row = (wa + wl + i for i in range(4))
    n_groups = d_lru // V7X_MXU_DIM
    g = pl.program_id(0)
    chunk = g % n_chunks
    seq0 = 0 if blocked_state else (g // n_chunks) * n_seq

    @pl.when(chunk == 0)
    def _():
        for k in range(wa - 1):
            row = jnp.zeros((n_seq, d_conv), jnp.float32) if zero_state else sa_ref[k]
            ext_a[:, HIST_ROWS - (wa - 1) + k, :] = row
        for k in range(wl - 1):
            row = jnp.zeros((n_seq, d_lru), jnp.float32) if zero_state else sl_ref[k]
            ext_l[:, HIST_ROWS - (wl - 1) + k, :] = row
        h_c[...] = jnp.zeros(h_c.shape, jnp.float32) if zero_state else sh_ref[...]

    x = x_ref[...].reshape(rows, d_model)
    hn = _rms_scale(x).astype(jnp.bfloat16)
    x_l = jnp.dot(hn, w_in_ref[:, 3 * d_conv:3 * d_conv + d_lru],
                  preferred_element_type=jnp.float32)

    ext_l[:, HIST_ROWS:, :] = x_l.reshape(n_seq, seg, d_lru)
    c = _causal_conv(ext_l, p_ref, lcw_row, wl, n_seq, seg) + p_ref[lcb_row:lcb_row + 1, :]
    c_bf = c.astype(jnp.bfloat16)
    gcxa_proj = jnp.dot(hn, w_in_ref[:, d_conv:3 * d_conv], preferred_element_type=jnp.float32)
    gates = [jnp.dot(c_bf[:, p * V7X_MXU_DIM:(p + 1) * V7X_MXU_DIM], wg_ref[p],
                     preferred_element_type=jnp.float32) for p in range(n_groups)]
    g_l = jnp.dot(hn, w_in_ref[:, 3 * d_conv + d_lru:], preferred_element_type=jnp.float32)
    g_b = jnp.dot(hn, w_in_ref[:, :d_conv], preferred_element_type=jnp.float32)
    neg_rate = -LRU_C * _softplus(p_ref[ap_row:ap_row + 1, :])
    for p in range(n_groups):
        cols = slice(p * V7X_MXU_DIM, (p + 1) * V7X_MXU_DIM)
        r_gate = _sigmoid(gates[p][:, :V7X_MXU_DIM] + p_ref[ba_row:ba_row + 1, cols])
        i_gate = _sigmoid(gates[p][:, V7X_MXU_DIM:] + p_ref[bx_row:bx_row + 1, cols])
        a = jnp.exp(r_gate * neg_rate[:, cols])
        one_m_a2 = 1.0 - a * a
        mult = jnp.where(one_m_a2 > 0.0, one_m_a2 * lax.rsqrt(one_m_a2), 0.0)
        ic = i_gate * c[:, cols]
        a_s[:, cols] = a
        b_s[:, cols] = mult * ic
        if reset_first:
            for s in range(n_seq):
                r0 = s * seg
                b_s[r0:r0 + 1, cols] = jnp.where(
                    chunk == 0, ic[r0:r0 + 1, :], (mult * ic)[r0:r0 + 1, :])

    assert seg % SCAN_BLOCK == 0
    h_in = None
    for r in range(0, rows, SCAN_BLOCK):
        s, t = divmod(r, seg)
        if t == 0:
            h_in = h_c[s:s + 1, :]
        a0, a1, a2, a3 = (a_s[r + k:r + k + 1, :] for k in range(SCAN_BLOCK))
        b0, b1, b2, b3 = (b_s[r + k:r + k + 1, :] for k in range(SCAN_BLOCK))
        a01 = a1 * a0
        b01 = a1 * b0 + b1
        a23 = a3 * a2
        b23 = a3 * b2 + b3
        h0 = a0 * h_in + b0
        h1 = a01 * h_in + b01
        h2 = a2 * h1 + b2
        h3 = (a23 * a01) * h_in + (a23 * b01 + b23)
        for k, h in enumerate((h0, h1, h2, h3)):
            h_s[r + k:r + k + 1, :] = h
        h_in = h3
        if t + SCAN_BLOCK == seg:
            h_c[s:s + 1, :] = h3

    g_c = gcxa_proj[:, :d_conv]
    x_a = gcxa_proj[:, d_conv:]
    ext_a[:, HIST_ROWS:, :] = (g_c * x_a).reshape(n_seq, seg, d_conv)
    out_a = g_b * _causal_conv(ext_a, p_ref, caw_row, wa, n_seq, seg)

    for k in range(wa - 1):
        na_ref[k, pl.ds(seq0, n_seq), :] = ext_a[:, seg + HIST_ROWS - (wa - 1) + k, :]
    ext_a[:, HIST_ROWS - (wa - 1):HIST_ROWS, :] = (
        ext_a[:, seg + HIST_ROWS - (wa - 1):seg + HIST_ROWS, :])
    for k in range(wl - 1):
        nl_ref[k, pl.ds(seq0, n_seq), :] = ext_l[:, seg + HIST_ROWS - (wl - 1) + k, :]
    ext_l[:, HIST_ROWS - (wl - 1):HIST_ROWS, :] = (
        ext_l[:, seg + HIST_ROWS - (wl - 1):seg + HIST_ROWS, :])
    nh_ref[pl.ds(seq0, n_seq), :] = h_c[...]

    out_b = h_s[...] * _gelu_tanh(g_l)
    mix = jnp.concatenate([out_a, out_b], axis=-1).astype(jnp.bfloat16)
    x1 = x + jnp.dot(mix, w_out_ref[...], preferred_element_type=jnp.float32)
    hm = _rms_scale(x1).astype(jnp.bfloat16)
    up = jnp.dot(hm, w_up_ref[...], preferred_element_type=jnp.float32)
    act = jnp.square(jnp.maximum(up, 0.0)).astype(jnp.bfloat16)
    x2 = x1 + jnp.dot(act, w_down_ref[...], preferred_element_type=jnp.float32)
    y_ref[...] = (_rms_scale(x2) * gf_ref[...]).reshape(n_seq, seg, d_model)


def _resident(shape):
    zeros = (0,) * len(shape)
    return pl.BlockSpec(shape, lambda g: zeros, pipeline_mode=pl.Buffered(1))


def _run_layer(x, states, weights, *, seq_per_block, chunk, conv_widths, reset_first, name):
    n_seq, t_len, d_model = x.shape
    wa, wl = conv_widths
    chans = weights[1].shape[-1]
    assert n_seq % seq_per_block == 0 and t_len % chunk == 0
    assert chunk % V7X_SUBLANES == 0
    rows = seq_per_block * chunk
    n_chunks = t_len // chunk
    n_tiles = (n_seq // seq_per_block) * n_chunks
    blocked_state = seq_per_block % V7X_SUBLANES == 0
    zero_state = states is None
    assert zero_state or blocked_state

    def rows_spec(n_rows):
        if blocked_state:
            return pl.BlockSpec((n_rows, seq_per_block, chans), lambda g: (0, g // n_chunks, 0))
        return pl.BlockSpec((n_rows, n_seq, chans), lambda g: (0, 0, 0))

    if blocked_state:
        h_spec = pl.BlockSpec((seq_per_block, chans), lambda g: (g // n_chunks, 0))
    else:
        h_spec = pl.BlockSpec((n_seq, chans), lambda g: (0, 0))
    x_spec = pl.BlockSpec((seq_per_block, chunk, d_model),
                          lambda g: (g // n_chunks, g % n_chunks, 0))

    operands = [x]
    in_specs = [x_spec]
    if not zero_state:
        operands += list(states)
        in_specs += [rows_spec(wa - 1), rows_spec(wl - 1), h_spec]
    operands += list(weights)
    in_specs += [_resident(w.shape) for w in weights]
    f32 = jnp.float32
    out_shape = (
        jax.ShapeDtypeStruct(x.shape, f32),
        jax.ShapeDtypeStruct((wa - 1, n_seq, chans), f32),
        jax.ShapeDtypeStruct((wl - 1, n_seq, chans), f32),
        jax.ShapeDtypeStruct((n_seq, chans), f32),
    )
    out_specs = (x_spec, rows_spec(wa - 1), rows_spec(wl - 1), h_spec)
    scratch = [
        pltpu.VMEM((seq_per_block, HIST_ROWS + chunk, chans), f32),
        pltpu.VMEM((seq_per_block, HIST_ROWS + chunk, chans), f32),
        pltpu.VMEM((seq_per_block, chans), f32),
        pltpu.VMEM((rows, chans), f32),
        pltpu.VMEM((rows, chans), f32),
        pltpu.VMEM((rows, chans), f32),
    ]
    return pl.pallas_call(
        functools.partial(
            _layer_kernel, n_chunks=n_chunks, conv_widths=conv_widths,
            reset_first=reset_first, zero_state=zero_state, blocked_state=blocked_state),
        grid=(n_tiles,),
        in_specs=in_specs,
        out_specs=out_specs,
        out_shape=out_shape,
        scratch_shapes=scratch,
        compiler_params=pltpu.CompilerParams(
            dimension_semantics=("arbitrary",),
            vmem_limit_bytes=V7X_VMEM_LIMIT_BYTES),
        name=name,
    )(*operands)


def _block_diag(w):
    n_heads, d, _ = w.shape
    eye = jnp.eye(n_heads, dtype=w.dtype)
    return (w[:, :, None, :] * eye[:, None, :, None]).reshape(n_heads * d, n_heads * d)


def _gate_weights(wa, wx):
    bd_a, bd_x = _block_diag(wa), _block_diag(wx)
    d_lru = bd_a.shape[0]
    groups = []
    for p in range(d_lru // V7X_MXU_DIM):
        sl = slice(p * V7X_MXU_DIM, (p + 1) * V7X_MXU_DIM)
        groups.append(jnp.concatenate([bd_a[sl, sl], bd_x[sl, sl]], axis=1))
    return jnp.stack(groups).astype(jnp.bfloat16)


def kernel(x_prompt, x_sample, state_conv_a, state_lru_conv, state_lru_h, norm1_g, w_in, conv_a_w, lru_conv_w, lru_conv_b, lru_wa, lru_ba, lru_wx, lru_bx, lru_a_param, w_out, norm2_g, w_up, w_down, norm_f_g):
    depth = w_in.shape[0]
    assert depth == 1
    dec_seq = x_sample.shape[1]
    d_conv = conv_a_w.shape[-1]
    d_lru = lru_conv_w.shape[-1]
    assert d_conv == d_lru
    conv_widths = (conv_a_w.shape[1], lru_conv_w.shape[1])
    bf16 = jnp.bfloat16
    l = 0
    small = jnp.concatenate([
        conv_a_w[l], lru_conv_w[l], lru_conv_b[l][None, :], lru_ba[l].reshape(1, d_lru),
        lru_bx[l].reshape(1, d_lru), lru_a_param[l][None, :]], axis=0)
    pad_rows = -small.shape[0] % V7X_SUBLANES
    small = jnp.pad(small, ((0, pad_rows), (0, 0)))
    weights = (
        (norm1_g[l][:, None] * w_in[l]).astype(bf16), small,
        _gate_weights(lru_wa[l], lru_wx[l]), w_out[l].astype(bf16),
        (norm2_g[l][:, None] * w_up[l]).astype(bf16), w_down[l].astype(bf16),
        norm_f_g[None, :],
    )

    def rows_first(a):
        return jnp.transpose(a, (1, 0, 2))

    yp, pa, pl_, ph = _run_layer(
        x_prompt, None, weights, seq_per_block=1, chunk=TILE_ROWS,
        conv_widths=conv_widths, reset_first=True, name="layer_prompt")
    sample_states = (rows_first(state_conv_a[l]), rows_first(state_lru_conv[l]),
                     state_lru_h[l])
    ys, sa, sl, sh = _run_layer(
        x_sample, sample_states, weights, seq_per_block=TILE_ROWS // dec_seq,
        chunk=dec_seq, conv_widths=conv_widths, reset_first=False, name="layer_sample")
    return (yp, ys, rows_first(pa)[None], rows_first(pl_)[None], ph[None],
            rows_first(sa)[None], rows_first(sl)[None], sh[None])
```

```python
import functools

import jax
import jax.numpy as jnp
from jax import lax
from jax.experimental import pallas as pl
from jax.experimental.pallas import tpu as pltpu

LRU_C = 8.0
EPS = 1e-6
GELU_K0 = 0.7978845608028654
GELU_K1 = 0.044715

V7X_SUBLANES = 8
V7X_MXU_DIM = 256
V7X_VMEM_LIMIT_BYTES = 58 * 1024 * 1024
HIST_ROWS = V7X_SUBLANES
TILE_ROWS = 512
SCAN_BLOCK = 4


def _rms_scale(x):
    return x * lax.rsqrt(jnp.mean(x * x, axis=-1, keepdims=True) + EPS)


def _gelu_tanh(x):
    half_x = 0.5 * x
    return half_x + half_x * jnp.tanh(x * (GELU_K0 + (GELU_K0 * GELU_K1) * (x * x)))


def _softplus(x):
    return jnp.maximum(x, 0.0) + jnp.log1p(jnp.exp(-jnp.abs(x)))


def _causal_conv(ext, p_ref, w_row, width, n_seq, seg):
    chans = ext.shape[-1]
    acc = None
    for k in reversed(range(width)):
        lo = HIST_ROWS - (width - 1) + k
        tap = p_ref[w_row + k:w_row + k + 1, :]
        term = ext[:, lo:lo + seg, :].reshape(n_seq * seg, chans) * tap
        acc = term if acc is None else acc + term
    return acc


def _layer_kernel(*refs, n_chunks, conv_widths, reset_first, zero_state, blocked_state):
    refs = list(refs)
    x_ref = refs.pop(0)
    if not zero_state:
        sa_ref, sl_ref, sh_ref = refs[:3]
        del refs[:3]
    (w_in_ref, p_ref, wg_ref, w_out_ref, w_up_ref, w_down_ref, gf_ref,
     y_ref, na_ref, nl_ref, nh_ref,
     ext_a, ext_l, h_c, a_s, b_s, h_s) = refs
    n_seq, seg, d_model = x_ref.shape
    rows = n_seq * seg
    d_conv = ext_a.shape[-1]
    d_lru = ext_l.shape[-1]
    wa, wl = conv_widths
    caw_row, lcw_row = 0, wa
    lcb_row, ba_row, bx_row, ap_row = (wa + wl + i for i in range(4))
    n_groups = d_lru // V7X_MXU_DIM
    g = pl.program_id(0)
    chunk = g % n_chunks
    seq0 = 0 if blocked_state else (g // n_chunks) * n_seq

    @pl.when(chunk == 0)
    def _():
        for k in range(wa - 1):
            row = jnp.zeros((n_seq, d_conv), jnp.float32) if zero_state else sa_ref[k]
            ext_a[:, HIST_ROWS - (wa - 1) + k, :] = row
        for k in range(wl - 1):
            row = jnp.zeros((n_seq, d_lru), jnp.float32) if zero_state else sl_ref[k]
            ext_l[:, HIST_ROWS - (wl - 1) + k, :] = row
        h_c[...] = jnp.zeros(h_c.shape, jnp.float32) if zero_state else sh_ref[...]

    x = x_ref[...].reshape(rows, d_model)
    hn = _rms_scale(x).astype(jnp.bfloat16)
    x_l = jnp.dot(hn, w_in_ref[:, 3 * d_conv:3 * d_conv + d_lru],
                  preferred_element_type=jnp.float32)

    ext_l[:, HIST_ROWS:, :] = x_l.reshape(n_seq, seg, d_lru)
    c = _causal_conv(ext_l, p_ref, lcw_row, wl, n_seq, seg) + p_ref[lcb_row:lcb_row + 1, :]
    c_bf = c.astype(jnp.bfloat16)
    gcxa_proj = jnp.dot(hn, w_in_ref[:, d_conv:3 * d_conv], preferred_element_type=jnp.float32)
    gates = [jnp.dot(c_bf[:, p * V7X_MXU_DIM:(p + 1) * V7X_MXU_DIM], wg_ref[p],
                     preferred_element_type=jnp.float32) for p in range(n_groups)]
    g_l = jnp.dot(hn, w_in_ref[:, 3 * d_conv + d_lru:], preferred_element_type=jnp.float32)
    g_b = jnp.dot(hn, w_in_ref[:, :d_conv], preferred_element_type=jnp.float32)
    half_rate = (-0.5 * LRU_C) * _softplus(p_ref[ap_row:ap_row + 1, :])
    half_ba = 0.5 * p_ref[ba_row:ba_row + 1, :]
    half_bx = 0.5 * p_ref[bx_row:bx_row + 1, :]
    for p in range(n_groups):
        cols = slice(p * V7X_MXU_DIM, (p + 1) * V7X_MXU_DIM)
        t_r = jnp.tanh(gates[p][:, :V7X_MXU_DIM] + half_ba[:, cols])
        t_i = jnp.tanh(gates[p][:, V7X_MXU_DIM:] + half_bx[:, cols])
        i_gate = 0.5 * t_i + 0.5
        a = jnp.exp(t_r * half_rate[:, cols] + half_rate[:, cols])
        one_m_a2 = 1.0 - a * a
        mult = jnp.where(one_m_a2 > 0.0, one_m_a2 * lax.rsqrt(one_m_a2), 0.0)
        ic = i_gate * c[:, cols]
        a_s[:, cols] = a
        b_s[:, cols] = mult * ic
        if reset_first:
            for s in range(n_seq):
                r0 = s * seg
                b_s[r0:r0 + 1, cols] = jnp.where(
                    chunk == 0, ic[r0:r0 + 1, :], (mult * ic)[r0:r0 + 1, :])

    assert seg % SCAN_BLOCK == 0
    h_in = None
    for r in range(0, rows, SCAN_BLOCK):
        s, t = divmod(r, seg)
        if t == 0:
            h_in = h_c[s:s + 1, :]
        a0, a1, a2, a3 = (a_s[r + k:r + k + 1, :] for k in range(SCAN_BLOCK))
        b0, b1, b2, b3 = (b_s[r + k:r + k + 1, :] for k in range(SCAN_BLOCK))
        a01 = a1 * a0
        b01 = a1 * b0 + b1
        a23 = a3 * a2
        b23 = a3 * b2 + b3
        h0 = a0 * h_in + b0
        h1 = a01 * h_in + b01
        h2 = a2 * h1 + b2
        h3 = (a23 * a01) * h_in + (a23 * b01 + b23)
        for k, h in enumerate((h0, h1, h2, h3)):
            h_s[r + k:r + k + 1, :] = h
        h_in = h3
        if t + SCAN_BLOCK == seg:
            h_c[s:s + 1, :] = h3

    g_c = gcxa_proj[:, :d_conv]
    x_a = gcxa_proj[:, d_conv:]
    ext_a[:, HIST_ROWS:, :] = (g_c * x_a).reshape(n_seq, seg, d_conv)
    out_a = g_b * _causal_conv(ext_a, p_ref, caw_row, wa, n_seq, seg)

    for k in range(wa - 1):
        na_ref[k, pl.ds(seq0, n_seq), :] = ext_a[:, seg + HIST_ROWS - (wa - 1) + k, :]
    ext_a[:, HIST_ROWS - (wa - 1):HIST_ROWS, :] = (
        ext_a[:, seg + HIST_ROWS - (wa - 1):seg + HIST_ROWS, :])
    for k in range(wl - 1):
        nl_ref[k, pl.ds(seq0, n_seq), :] = ext_l[:, seg + HIST_ROWS - (wl - 1) + k, :]
    ext_l[:, HIST_ROWS - (wl - 1):HIST_ROWS, :] = (
        ext_l[:, seg + HIST_ROWS - (wl - 1):seg + HIST_ROWS, :])
    nh_ref[pl.ds(seq0, n_seq), :] = h_c[...]

    out_b = h_s[...] * _gelu_tanh(g_l)
    mix = jnp.concatenate([out_a, out_b], axis=-1).astype(jnp.bfloat16)
    x1 = x + jnp.dot(mix, w_out_ref[...], preferred_element_type=jnp.float32)
    hm = _rms_scale(x1).astype(jnp.bfloat16)
    up = jnp.dot(hm, w_up_ref[...], preferred_element_type=jnp.float32)
    act = jnp.square(jnp.maximum(up, 0.0)).astype(jnp.bfloat16)
    x2 = x1 + jnp.dot(act, w_down_ref[...], preferred_element_type=jnp.float32)
    y_ref[...] = (_rms_scale(x2) * gf_ref[...]).reshape(n_seq, seg, d_model)


def _resident(shape):
    zeros = (0,) * len(shape)
    return pl.BlockSpec(shape, lambda g: zeros, pipeline_mode=pl.Buffered(1))


def _run_layer(x, states, weights, *, seq_per_block, chunk, conv_widths, reset_first, name):
    n_seq, t_len, d_model = x.shape
    wa, wl = conv_widths
    chans = weights[1].shape[-1]
    assert n_seq % seq_per_block == 0 and t_len % chunk == 0
    assert chunk % V7X_SUBLANES == 0
    rows = seq_per_block * chunk
    n_chunks = t_len // chunk
    n_tiles = (n_seq // seq_per_block) * n_chunks
    blocked_state = seq_per_block % V7X_SUBLANES == 0
    zero_state = states is None
    assert zero_state or blocked_state

    def rows_spec(n_rows):
        if blocked_state:
            return pl.BlockSpec((n_rows, seq_per_block, chans), lambda g: (0, g // n_chunks, 0))
        return pl.BlockSpec((n_rows, n_seq, chans), lambda g: (0, 0, 0))

    if blocked_state:
        h_spec = pl.BlockSpec((seq_per_block, chans), lambda g: (g // n_chunks, 0))
    else:
        h_spec = pl.BlockSpec((n_seq, chans), lambda g: (0, 0))
    x_spec = pl.BlockSpec((seq_per_block, chunk, d_model),
                          lambda g: (g // n_chunks, g % n_chunks, 0))

    operands = [x]
    in_specs = [x_spec]
    if not zero_state:
        operands += list(states)
        in_specs += [rows_spec(wa - 1), rows_spec(wl - 1), h_spec]
    operands += list(weights)
    in_specs += [_resident(w.shape) for w in weights]
    f32 = jnp.float32
    out_shape = (
        jax.ShapeDtypeStruct(x.shape, f32),
        jax.ShapeDtypeStruct((wa - 1, n_seq, chans), f32),
        jax.ShapeDtypeStruct((wl - 1, n_seq, chans), f32),
        jax.ShapeDtypeStruct((n_seq, chans), f32),
    )
    out_specs = (x_spec, rows_spec(wa - 1), rows_spec(wl - 1), h_spec)
    scratch = [
        pltpu.VMEM((seq_per_block, HIST_ROWS + chunk, chans), f32),
        pltpu.VMEM((seq_per_block, HIST_ROWS + chunk, chans), f32),
        pltpu.VMEM((seq_per_block, chans), f32),
        pltpu.VMEM((rows, chans), f32),
        pltpu.VMEM((rows, chans), f32),
        pltpu.VMEM((rows, chans), f32),
    ]
    return pl.pallas_call(
        functools.partial(
            _layer_kernel, n_chunks=n_chunks, conv_widths=conv_widths,
            reset_first=reset_first, zero_state=zero_state, blocked_state=blocked_state),
        grid=(n_tiles,),
        in_specs=in_specs,
        out_specs=out_specs,
        out_shape=out_shape,
        scratch_shapes=scratch,
        compiler_params=pltpu.CompilerParams(
            dimension_semantics=("arbitrary",),
            vmem_limit_bytes=V7X_VMEM_LIMIT_BYTES),
        name=name,
    )(*operands)


def _block_diag(w):
    n_heads, d, _ = w.shape
    eye = jnp.eye(n_heads, dtype=w.dtype)
    return (w[:, :, None, :] * eye[:, None, :, None]).reshape(n_heads * d, n_heads * d)


def _gate_weights(wa, wx):
    bd_a, bd_x = _block_diag(wa), _block_diag(wx)
    d_lru = bd_a.shape[0]
    groups = []
    for p in range(d_lru // V7X_MXU_DIM):
        sl = slice(p * V7X_MXU_DIM, (p + 1) * V7X_MXU_DIM)
        groups.append(jnp.concatenate([bd_a[sl, sl], bd_x[sl, sl]], axis=1))
    return (0.5 * jnp.stack(groups)).astype(jnp.bfloat16)


def kernel(x_prompt, x_sample, state_conv_a, state_lru_conv, state_lru_h, norm1_g, w_in, conv_a_w, lru_conv_w, lru_conv_b, lru_wa, lru_ba, lru_wx, lru_bx, lru_a_param, w_out, norm2_g, w_up, w_down, norm_f_g):
    depth = w_in.shape[0]
    assert depth == 1
    dec_seq = x_sample.shape[1]
    d_conv = conv_a_w.shape[-1]
    d_lru = lru_conv_w.shape[-1]
    assert d_conv == d_lru
    conv_widths = (conv_a_w.shape[1], lru_conv_w.shape[1])
    bf16 = jnp.bfloat16
    l = 0
    small = jnp.concatenate([
        conv_a_w[l], lru_conv_w[l], lru_conv_b[l][None, :], lru_ba[l].reshape(1, d_lru),
        lru_bx[l].reshape(1, d_lru), lru_a_param[l][None, :]], axis=0)
    pad_rows = -small.shape[0] % V7X_SUBLANES
    small = jnp.pad(small, ((0, pad_rows), (0, 0)))
    weights = (
        (norm1_g[l][:, None] * w_in[l]).astype(bf16), small,
        _gate_weights(lru_wa[l], lru_wx[l]), w_out[l].astype(bf16),
        (norm2_g[l][:, None] * w_up[l]).astype(bf16), w_down[l].astype(bf16),
        norm_f_g[None, :],
    )

    def rows_first(a):
        return jnp.transpose(a, (1, 0, 2))

    yp, pa, pl_, ph = _run_layer(
        x_prompt, None, weights, seq_per_block=1, chunk=TILE_ROWS,
        conv_widths=conv_widths, reset_first=True, name="layer_prompt")
    sample_states = (rows_first(state_conv_a[l]), rows_first(state_lru_conv[l]),
                     state_lru_h[l])
    ys, sa, sl, sh = _run_layer(
        x_sample, sample_states, weights, seq_per_block=TILE_ROWS // dec_seq,
        chunk=dec_seq, conv_widths=conv_widths, reset_first=False, name="layer_sample")
    return (yp, ys, rows_first(pa)[None], rows_first(pl_)[None], ph[None],
            rows_first(sa)[None], rows_first(sl)[None], sh[None])
```

```python
import functools

import jax
import jax.numpy as jnp
from jax import lax
from jax.experimental import pallas as pl
from jax.experimental.pallas import tpu as pltpu

LRU_C = 8.0
EPS = 1e-6
GELU_K0 = 0.7978845608028654
GELU_K1 = 0.044715

V7X_SUBLANES = 8
V7X_MXU_DIM = 256
V7X_VMEM_LIMIT_BYTES = 58 * 1024 * 1024
HIST_ROWS = V7X_SUBLANES
TILE_ROWS = 512
SCAN_BLOCK = 4


def _rms_scale(x):
    return x * lax.rsqrt(jnp.mean(x * x, axis=-1, keepdims=True) + EPS)


def _sigmoid(x):
    return 0.5 * jnp.tanh(0.5 * x) + 0.5


def _gelu_tanh(x):
    return 0.5 * x * (1.0 + jnp.tanh(GELU_K0 * (x + GELU_K1 * (x * x * x))))


def _softplus(x):
    return jnp.maximum(x, 0.0) + jnp.log1p(jnp.exp(-jnp.abs(x)))


def _causal_conv(ext, p_ref, w_row, width, n_seq, seg):
    chans = ext.shape[-1]
    acc = None
    for k in reversed(range(width)):
        lo = HIST_ROWS - (width - 1) + k
        tap = p_ref[w_row + k:w_row + k + 1, :]
        term = ext[:, lo:lo + seg, :].reshape(n_seq * seg, chans) * tap
        acc = term if acc is None else acc + term
    return acc


def _layer_kernel(*refs, n_chunks, conv_widths, reset_first, zero_state, blocked_state):
    refs = list(refs)
    x_ref = refs.pop(0)
    if not zero_state:
        sa_ref, sl_ref, sh_ref = refs[:3]
        del refs[:3]
    (w_in_ref, p_ref, wg_ref, w_out_ref, w_up_ref, w_down_ref, gf_ref,
     y_ref, na_ref, nl_ref, nh_ref,
     ext_a, ext_l, h_c, a_s, b_s, h_s) = refs
    n_seq, seg, d_model = x_ref.shape
    rows = n_seq * seg
    d_conv = ext_a.shape[-1]
    d_lru = ext_l.shape[-1]
    wa, wl = conv_widths
    caw_row, lcw_row = 0, wa
    lcb_row, ba_row, bx_row, ap_row = (wa + wl + i for i in range(4))
    n_groups = d_lru // V7X_MXU_DIM
    g = pl.program_id(0)
    chunk = g % n_chunks
    seq0 = 0 if blocked_state else (g // n_chunks) * n_seq

    @pl.when(chunk == 0)
    def _():
        for k in range(wa - 1):
            row = jnp.zeros((n_seq, d_conv), jnp.float32) if zero_state else sa_ref[k]
            ext_a[:, HIST_ROWS - (wa - 1) + k, :] = row
        for k in range(wl - 1):
            row = jnp.zeros((n_seq, d_lru), jnp.float32) if zero_state else sl_ref[k]
            ext_l[:, HIST_ROWS - (wl - 1) + k, :] = row
        h_c[...] = jnp.zeros(h_c.shape, jnp.float32) if zero_state else sh_ref[...]

    x = x_ref[...].reshape(rows, d_model)
    hn = _rms_scale(x).astype(jnp.bfloat16)
    x_l = jnp.dot(hn, w_in_ref[:, 3 * d_conv:3 * d_conv + d_lru],
                  preferred_element_type=jnp.float32)

    ext_l[:, HIST_ROWS:, :] = x_l.reshape(n_seq, seg, d_lru)
    c = _causal_conv(ext_l, p_ref, lcw_row, wl, n_seq, seg) + p_ref[lcb_row:lcb_row + 1, :]
    c_bf = c.astype(jnp.bfloat16)
    gcxa_proj = jnp.dot(hn, w_in_ref[:, d_conv:3 * d_conv], preferred_element_type=jnp.float32)
    gates = [jnp.dot(c_bf[:, p * V7X_MXU_DIM:(p + 1) * V7X_MXU_DIM], wg_ref[p],
                     preferred_element_type=jnp.float32) for p in range(n_groups)]
    g_l = jnp.dot(hn, w_in_ref[:, 3 * d_conv + d_lru:], preferred_element_type=jnp.float32)
    g_b = jnp.dot(hn, w_in_ref[:, :d_conv], preferred_element_type=jnp.float32)
    half_rate = (-0.5 * LRU_C) * _softplus(p_ref[ap_row:ap_row + 1, :])
    half_ba = 0.5 * p_ref[ba_row:ba_row + 1, :]
    half_bx = 0.5 * p_ref[bx_row:bx_row + 1, :]
    for p in range(n_groups):
        cols = slice(p * V7X_MXU_DIM, (p + 1) * V7X_MXU_DIM)
        t_r = jnp.tanh(gates[p][:, :V7X_MXU_DIM] + half_ba[:, cols])
        t_i = jnp.tanh(gates[p][:, V7X_MXU_DIM:] + half_bx[:, cols])
        i_gate = 0.5 * t_i + 0.5
        a = jnp.exp(t_r * half_rate[:, cols] + half_rate[:, cols])
        one_m_a2 = 1.0 - a * a
        mult = jnp.where(one_m_a2 > 0.0, one_m_a2 * lax.rsqrt(one_m_a2), 0.0)
        ic = i_gate * c[:, cols]
        a_s[:, cols] = a
        b_s[:, cols] = mult * ic
        if reset_first:
            for s in range(n_seq):
                r0 = s * seg
                b_s[r0:r0 + 1, cols] = jnp.where(
                    chunk == 0, ic[r0:r0 + 1, :], (mult * ic)[r0:r0 + 1, :])

    assert seg % SCAN_BLOCK == 0
    h_in = None
    for r in range(0, rows, SCAN_BLOCK):
        s, t = divmod(r, seg)
        if t == 0:
            h_in = h_c[s:s + 1, :]
        a0, a1, a2, a3 = (a_s[r + k:r + k + 1, :] for k in range(SCAN_BLOCK))
        b0, b1, b2, b3 = (b_s[r + k:r + k + 1, :] for k in range(SCAN_BLOCK))
        a01 = a1 * a0
        b01 = a1 * b0 + b1
        a23 = a3 * a2
        b23 = a3 * b2 + b3
        h0 = a0 * h_in + b0
        h1 = a01 * h_in + b01
        h2 = a2 * h1 + b2
        h3 = (a23 * a01) * h_in + (a23 * b01 + b23)
        for k, h in enumerate((h0, h1, h2, h3)):
            h_s[r + k:r + k + 1, :] = h
        h_in = h3
        if t + SCAN_BLOCK == seg:
            h_c[s:s + 1, :] = h3

    g_c = gcxa_proj[:, :d_conv]
    x_a = gcxa_proj[:, d_conv:]
    ext_a[:, HIST_ROWS:, :] = (g_c * x_a).reshape(n_seq, seg, d_conv)
    out_a = g_b * _causal_conv(ext_a, p_ref, caw_row, wa, n_seq, seg)

    for k in range(wa - 1):
        na_ref[k, pl.ds(seq0, n_seq), :] = ext_a[:, seg + HIST_ROWS - (wa - 1) + k, :]
    ext_a[:, HIST_ROWS - (wa - 1):HIST_ROWS, :] = (
        ext_a[:, seg + HIST_ROWS - (wa - 1):seg + HIST_ROWS, :])
    for k in range(wl - 1):
        nl_ref[k, pl.ds(seq0, n_seq), :] = ext_l[:, seg + HIST_ROWS - (wl - 1) + k, :]
    ext_l[:, HIST_ROWS - (wl - 1):HIST_ROWS, :] = (
        ext_l[:, seg + HIST_ROWS - (wl - 1):seg + HIST_ROWS, :])
    nh_ref[pl.ds(seq0, n_seq), :] = h_c[...]

    out_b = h_s[...] * _gelu_tanh(g_l)
    mix = jnp.concatenate([out_a, out_b], axis=-1).astype(jnp.bfloat16)
    x1 = x + jnp.dot(mix, w_out_ref[...], preferred_element_type=jnp.float32)
    hm = _rms_scale(x1).astype(jnp.bfloat16)
    up = jnp.dot(hm, w_up_ref[...], preferred_element_type=jnp.float32)
    act = jnp.square(jnp.maximum(up, 0.0)).astype(jnp.bfloat16)
    x2 = x1 + jnp.dot(act, w_down_ref[...], preferred_element_type=jnp.float32)
    y_ref[...] = (_rms_scale(x2) * gf_ref[...]).reshape(n_seq, seg, d_model)


def _resident(shape):
    zeros = (0,) * len(shape)
    return pl.BlockSpec(shape, lambda g: zeros, pipeline_mode=pl.Buffered(1))


def _run_layer(x, states, weights, *, seq_per_block, chunk, conv_widths, reset_first, name):
    n_seq, t_len, d_model = x.shape
    wa, wl = conv_widths
    chans = weights[1].shape[-1]
    assert n_seq % seq_per_block == 0 and t_len % chunk == 0
    assert chunk % V7X_SUBLANES == 0
    rows = seq_per_block * chunk
    n_chunks = t_len // chunk
    n_tiles = (n_seq // seq_per_block) * n_chunks
    blocked_state = seq_per_block % V7X_SUBLANES == 0
    zero_state = states is None
    assert zero_state or blocked_state

    def rows_spec(n_rows):
        if blocked_state:
            return pl.BlockSpec((n_rows, seq_per_block, chans), lambda g: (0, g // n_chunks, 0))
        return pl.BlockSpec((n_rows, n_seq, chans), lambda g: (0, 0, 0))

    if blocked_state:
        h_spec = pl.BlockSpec((seq_per_block, chans), lambda g: (g // n_chunks, 0))
    else:
        h_spec = pl.BlockSpec((n_seq, chans), lambda g: (0, 0))
    x_spec = pl.BlockSpec((seq_per_block, chunk, d_model),
                          lambda g: (g // n_chunks, g % n_chunks, 0))

    operands = [x]
    in_specs = [x_spec]
    if not zero_state:
        operands += list(states)
        in_specs += [rows_spec(wa - 1), rows_spec(wl - 1), h_spec]
    operands += list(weights)
    in_specs += [_resident(w.shape) for w in weights]
    f32 = jnp.float32
    out_shape = (
        jax.ShapeDtypeStruct(x.shape, f32),
        jax.ShapeDtypeStruct((wa - 1, n_seq, chans), f32),
        jax.ShapeDtypeStruct((wl - 1, n_seq, chans), f32),
        jax.ShapeDtypeStruct((n_seq, chans), f32),
    )
    out_specs = (x_spec, rows_spec(wa - 1), rows_spec(wl - 1), h_spec)
    scratch = [
        pltpu.VMEM((seq_per_block, HIST_ROWS + chunk, chans), f32),
        pltpu.VMEM((seq_per_block, HIST_ROWS + chunk, chans), f32),
        pltpu.VMEM((seq_per_block, chans), f32),
        pltpu.VMEM((rows, chans), f32),
        pltpu.VMEM((rows, chans), f32),
        pltpu.VMEM((rows, chans), f32),
    ]
    return pl.pallas_call(
        functools.partial(
            _layer_kernel, n_chunks=n_chunks, conv_widths=conv_widths,
            reset_first=reset_first, zero_state=zero_state, blocked_state=blocked_state),
        grid=(n_tiles,),
        in_specs=in_specs,
        out_specs=out_specs,
        out_shape=out_shape,
        scratch_shapes=scratch,
        compiler_params=pltpu.CompilerParams(
            dimension_semantics=("arbitrary",),
            vmem_limit_bytes=V7X_VMEM_LIMIT_BYTES),
        name=name,
    )(*operands)


def _block_diag(w):
    n_heads, d, _ = w.shape
    eye = jnp.eye(n_heads, dtype=w.dtype)
    return (w[:, :, None, :] * eye[:, None, :, None]).reshape(n_heads * d, n_heads * d)


def _gate_weights(wa, wx):
    bd_a, bd_x = _block_diag(wa), _block_diag(wx)
    d_lru = bd_a.shape[0]
    groups = []
    for p in range(d_lru // V7X_MXU_DIM):
        sl = slice(p * V7X_MXU_DIM, (p + 1) * V7X_MXU_DIM)
        groups.append(jnp.concatenate([bd_a[sl, sl], bd_x[sl, sl]], axis=1))
    return (0.5 * jnp.stack(groups)).astype(jnp.bfloat16)


def kernel(x_prompt, x_sample, state_conv_a, state_lru_conv, state_lru_h, norm1_g, w_in, conv_a_w, lru_conv_w, lru_conv_b, lru_wa, lru_ba, lru_wx, lru_bx, lru_a_param, w_out, norm2_g, w_up, w_down, norm_f_g):
    depth = w_in.shape[0]
    assert depth == 1
    dec_seq = x_sample.shape[1]
    d_conv = conv_a_w.shape[-1]
    d_lru = lru_conv_w.shape[-1]
    assert d_conv == d_lru
    conv_widths = (conv_a_w.shape[1], lru_conv_w.shape[1])
    bf16 = jnp.bfloat16
    l = 0
    small = jnp.concatenate([
        conv_a_w[l], lru_conv_w[l], lru_conv_b[l][None, :], lru_ba[l].reshape(1, d_lru),
        lru_bx[l].reshape(1, d_lru), lru_a_param[l][None, :]], axis=0)
    pad_rows = -small.shape[0] % V7X_SUBLANES
    small = jnp.pad(small, ((0, pad_rows), (0, 0)))
    weights = (
        (norm1_g[l][:, None] * w_in[l]).astype(bf16), small,
        _gate_weights(lru_wa[l], lru_wx[l]), w_out[l].astype(bf16),
        (norm2_g[l][:, None] * w_up[l]).astype(bf16), w_down[l].astype(bf16),
        norm_f_g[None, :],
    )

    def rows_first(a):
        return jnp.transpose(a, (1, 0, 2))

    yp, pa, pl_, ph = _run_layer(
        x_prompt, None, weights, seq_per_block=1, chunk=TILE_ROWS,
        conv_widths=conv_widths, reset_first=True, name="layer_prompt")
    sample_states = (rows_first(state_conv_a[l]), rows_first(state_lru_conv[l]),
                     state_lru_h[l])
    ys, sa, sl, sh = _run_layer(
        x_sample, sample_states, weights, seq_per_block=TILE_ROWS // dec_seq,
        chunk=dec_seq, conv_widths=conv_widths, reset_first=False, name="layer_sample")
    return (yp, ys, rows_first(pa)[None], rows_first(pl_)[None], ph[None],
            rows_first(sa)[None], rows_first(sl)[None], sh[None])
```

```python
import functools

import jax
import jax.numpy as jnp
from jax import lax
from jax.experimental import pallas as pl
from jax.experimental.pallas import tpu as pltpu

LRU_C = 8.0
EPS = 1e-6
GELU_K0 = 0.7978845608028654
GELU_K1 = 0.044715

V7X_SUBLANES = 8
V7X_MXU_DIM = 256
V7X_VMEM_LIMIT_BYTES = 58 * 1024 * 1024
HIST_ROWS = V7X_SUBLANES
TILE_ROWS = 512
SCAN_BLOCK = 4


def _rms_scale(x):
    return x * lax.rsqrt(jnp.mean(x * x, axis=-1, keepdims=True) + EPS)


def _gelu_tanh(x):
    return 0.5 * x * (1.0 + jnp.tanh(GELU_K0 * (x + GELU_K1 * (x * x * x))))


def _softplus(x):
    return jnp.maximum(x, 0.0) + jnp.log1p(jnp.exp(-jnp.abs(x)))


def _causal_conv(ext, p_ref, w_row, width, n_seq, seg):
    chans = ext.shape[-1]
    acc = None
    for k in reversed(range(width)):
        lo = HIST_ROWS - (width - 1) + k
        tap = p_ref[w_row + k:w_row + k + 1, :]
        term = ext[:, lo:lo + seg, :].reshape(n_seq * seg, chans) * tap
        acc = term if acc is None else acc + term
    return acc


def _layer_kernel(*refs, n_chunks, conv_widths, reset_first, zero_state, blocked_state):
    refs = list(refs)
    x_ref = refs.pop(0)
    if not zero_state:
        sa_ref, sl_ref, sh_ref = refs[:3]
        del refs[:3]
    (w_in_ref, p_ref, wg_ref, w_out_ref, w_up_ref, w_down_ref, gf_ref,
     y_ref, na_ref, nl_ref, nh_ref,
     ext_a, ext_l, h_c, a_s, b_s, h_s) = refs
    n_seq, seg, d_model = x_ref.shape
    rows = n_seq * seg
    d_conv = ext_a.shape[-1]
    d_lru = ext_l.shape[-1]
    wa, wl = conv_widths
    caw_row, lcw_row = 0, wa
    lcb_row, ba_row, bx_row, ap_row = (wa + wl + i for i in range(4))
    n_groups = d_lru // V7X_MXU_DIM
    g = pl.program_id(0)
    chunk = g % n_chunks
    seq0 = 0 if blocked_state else (g // n_chunks) * n_seq

    @pl.when(chunk == 0)
    def _():
        for k in range(wa - 1):
            row = jnp.zeros((n_seq, d_conv), jnp.float32) if zero_state else sa_ref[k]
            ext_a[:, HIST_ROWS - (wa - 1) + k, :] = row
        for k in range(wl - 1):
            row = jnp.zeros((n_seq, d_lru), jnp.float32) if zero_state else sl_ref[k]
            ext_l[:, HIST_ROWS - (wl - 1) + k, :] = row
        h_c[...] = jnp.zeros(h_c.shape, jnp.float32) if zero_state else sh_ref[...]

    x = x_ref[...].reshape(rows, d_model)
    hn = _rms_scale(x).astype(jnp.bfloat16)
    x_l = jnp.dot(hn, w_in_ref[:, 3 * d_conv:3 * d_conv + d_lru],
                  preferred_element_type=jnp.float32)

    ext_l[:, HIST_ROWS:, :] = x_l.reshape(n_seq, seg, d_lru)
    c = _causal_conv(ext_l, p_ref, lcw_row, wl, n_seq, seg) + p_ref[lcb_row:lcb_row + 1, :]
    c_bf = c.astype(jnp.bfloat16)
    gcxa_proj = jnp.dot(hn, w_in_ref[:, d_conv:3 * d_conv], preferred_element_type=jnp.float32)
    gates = [jnp.dot(c_bf[:, p * V7X_MXU_DIM:(p + 1) * V7X_MXU_DIM], wg_ref[p],
                     preferred_element_type=jnp.float32) for p in range(n_groups)]
    g_l = jnp.dot(hn, w_in_ref[:, 3 * d_conv + d_lru:], preferred_element_type=jnp.float32)
    g_b = jnp.dot(hn, w_in_ref[:, :d_conv], preferred_element_type=jnp.float32)
    half_rate = (-0.5 * LRU_C) * _softplus(p_ref[ap_row:ap_row + 1, :])
    half_ba = 0.5 * p_ref[ba_row:ba_row + 1, :]
    half_bx = 0.5 * p_ref[bx_row:bx_row + 1, :]
    for p in range(n_groups):
        cols = slice(p * V7X_MXU_DIM, (p + 1) * V7X_MXU_DIM)
        t_r = jnp.tanh(gates[p][:, :V7X_MXU_DIM] + half_ba[:, cols])
        t_i = jnp.tanh(gates[p][:, V7X_MXU_DIM:] + half_bx[:, cols])
        i_gate = 0.5 * t_i + 0.5
        a = jnp.exp(t_r * half_rate[:, cols] + half_rate[:, cols])
        one_m_a2 = 1.0 - a * a
        mult = jnp.where(one_m_a2 > 0.0, one_m_a2 * lax.rsqrt(one_m_a2), 0.0)
        ic = i_gate * c[:, cols]
        a_s[:, cols] = a
        b_s[:, cols] = mult * ic
        if reset_first:
            for s in range(n_seq):
                r0 = s * seg
                b_s[r0:r0 + 1, cols] = jnp.where(
                    chunk == 0, ic[r0:r0 + 1, :], (mult * ic)[r0:r0 + 1, :])

    assert seg % SCAN_BLOCK == 0
    h_in = None
    for r in range(0, rows, SCAN_BLOCK):
        s, t = divmod(r, seg)
        if t == 0:
            h_in = h_c[s:s + 1, :]
        a0, a1, a2, a3 = (a_s[r + k:r + k + 1, :] for k in range(SCAN_BLOCK))
        b0, b1, b2, b3 = (b_s[r + k:r + k + 1, :] for k in range(SCAN_BLOCK))
        a01 = a1 * a0
        b01 = a1 * b0 + b1
        a23 = a3 * a2
        b23 = a3 * b2 + b3
        h0 = a0 * h_in + b0
        h1 = a01 * h_in + b01
        h2 = a2 * h1 + b2
        h3 = (a23 * a01) * h_in + (a23 * b01 + b23)
        for k, h in enumerate((h0, h1, h2, h3)):
            h_s[r + k:r + k + 1, :] = h
        h_in = h3
        if t + SCAN_BLOCK == seg:
            h_c[s:s + 1, :] = h3

    g_c = gcxa_proj[:, :d_conv]
    x_a = gcxa_proj[:, d_conv:]
    ext_a[:, HIST_ROWS:, :] = (g_c * x_a).reshape(n_seq, seg, d_conv)
    out_a = g_b * _causal_conv(ext_a, p_ref, caw_row, wa, n_seq, seg)

    for k in range(wa - 1):
        na_ref[k, pl.ds(seq0, n_seq), :] = ext_a[:, seg + HIST_ROWS - (wa - 1) + k, :]
    ext_a[:, HIST_ROWS - (wa - 1):HIST_ROWS, :] = (
        ext_a[:, seg + HIST_ROWS - (wa - 1):seg + HIST_ROWS, :])
    for k in range(wl - 1):
        nl_ref[k, pl.ds(seq0, n_seq), :] = ext_l[:, seg + HIST_ROWS - (wl - 1) + k, :]
    ext_l[:, HIST_ROWS - (wl - 1):HIST_ROWS, :] = (
        ext_l[:, seg + HIST_ROWS - (wl - 1):seg + HIST_ROWS, :])
    nh_ref[pl.ds(seq0, n_seq), :] = h_c[...]

    out_b = h_s[...] * _gelu_tanh(g_l)
    mix = jnp.concatenate([out_a, out_b], axis=-1).astype(jnp.bfloat16)
    x1 = x + jnp.dot(mix, w_out_ref[...], preferred_element_type=jnp.float32)
    hm = _rms_scale(x1).astype(jnp.bfloat16)
    up = jnp.dot(hm, w_up_ref[...], preferred_element_type=jnp.float32)
    act = jnp.square(jnp.maximum(up, 0.0)).astype(jnp.bfloat16)
    x2 = x1 + jnp.dot(act, w_down_ref[...], preferred_element_type=jnp.float32)
    y_ref[...] = (_rms_scale(x2) * gf_ref[...]).reshape(n_seq, seg, d_model)


def _resident(shape):
    zeros = (0,) * len(shape)
    return pl.BlockSpec(shape, lambda g: zeros, pipeline_mode=pl.Buffered(1))


def _run_layer(x, states, weights, *, seq_per_block, chunk, conv_widths, reset_first, name):
    n_seq, t_len, d_model = x.shape
    wa, wl = conv_widths
    chans = weights[1].shape[-1]
    assert n_seq % seq_per_block == 0 and t_len % chunk == 0
    assert chunk % V7X_SUBLANES == 0
    rows = seq_per_block * chunk
    n_chunks = t_len // chunk
    n_tiles = (n_seq // seq_per_block) * n_chunks
    blocked_state = seq_per_block % V7X_SUBLANES == 0
    zero_state = states is None
    assert zero_state or blocked_state

    def rows_spec(n_rows):
        if blocked_state:
            return pl.BlockSpec((n_rows, seq_per_block, chans), lambda g: (0, g // n_chunks, 0))
        return pl.BlockSpec((n_rows, n_seq, chans), lambda g: (0, 0, 0))

    if blocked_state:
        h_spec = pl.BlockSpec((seq_per_block, chans), lambda g: (g // n_chunks, 0))
    else:
        h_spec = pl.BlockSpec((n_seq, chans), lambda g: (0, 0))
    x_spec = pl.BlockSpec((seq_per_block, chunk, d_model),
                          lambda g: (g // n_chunks, g % n_chunks, 0))

    operands = [x]
    in_specs = [x_spec]
    if not zero_state:
        operands += list(states)
        in_specs += [rows_spec(wa - 1), rows_spec(wl - 1), h_spec]
    operands += list(weights)
    in_specs += [_resident(w.shape) for w in weights]
    f32 = jnp.float32
    out_shape = (
        jax.ShapeDtypeStruct(x.shape, f32),
        jax.ShapeDtypeStruct((wa - 1, n_seq, chans), f32),
        jax.ShapeDtypeStruct((wl - 1, n_seq, chans), f32),
        jax.ShapeDtypeStruct((n_seq, chans), f32),
    )
    out_specs = (x_spec, rows_spec(wa - 1), rows_spec(wl - 1), h_spec)
    scratch = [
        pltpu.VMEM((seq_per_block, HIST_ROWS + chunk, chans), f32),
        pltpu.VMEM((seq_per_block, HIST_ROWS + chunk, chans), f32),
        pltpu.VMEM((seq_per_block, chans), f32),
        pltpu.VMEM((rows, chans), f32),
        pltpu.VMEM((rows, chans), f32),
        pltpu.VMEM((rows, chans), f32),
    ]
    return pl.pallas_call(
        functools.partial(
            _layer_kernel, n_chunks=n_chunks, conv_widths=conv_widths,
            reset_first=reset_first, zero_state=zero_state, blocked_state=blocked_state),
        grid=(n_tiles,),
        in_specs=in_specs,
        out_specs=out_specs,
        out_shape=out_shape,
        scratch_shapes=scratch,
        compiler_params=pltpu.CompilerParams(
            dimension_semantics=("arbitrary",),
            vmem_limit_bytes=V7X_VMEM_LIMIT_BYTES),
        name=name,
    )(*operands)


def _block_diag(w):
    n_heads, d, _ = w.shape
    eye = jnp.eye(n_heads, dtype=w.dtype)
    return (w[:, :, None, :] * eye[:, None, :, None]).reshape(n_heads * d, n_heads * d)


def _gate_weights(wa, wx):
    bd_a, bd_x = _block_diag(wa), _block_diag(wx)
    d_lru = bd_a.shape[0]
    groups = []
    for p in range(d_lru // V7X_MXU_DIM):
        sl = slice(p * V7X_MXU_DIM, (p + 1) * V7X_MXU_DIM)
        groups.append(jnp.concatenate([bd_a[sl, sl], bd_x[sl, sl]], axis=1))
    return (0.5 * jnp.stack(groups)).astype(jnp.bfloat16)


def kernel(x_prompt, x_sample, state_conv_a, state_lru_conv, state_lru_h, norm1_g, w_in, conv_a_w, lru_conv_w, lru_conv_b, lru_wa, lru_ba, lru_wx, lru_bx, lru_a_param, w_out, norm2_g, w_up, w_down, norm_f_g):
    depth = w_in.shape[0]
    assert depth == 1
    dec_seq = x_sample.shape[1]
    d_conv = conv_a_w.shape[-1]
    d_lru = lru_conv_w.shape[-1]
    assert d_conv == d_lru
    conv_widths = (conv_a_w.shape[1], lru_conv_w.shape[1])
    bf16 = jnp.bfloat16
    l = 0
    small = jnp.concatenate([
        conv_a_w[l], lru_conv_w[l], lru_conv_b[l][None, :], lru_ba[l].reshape(1, d_lru),
        lru_bx[l].reshape(1, d_lru), lru_a_param[l][None, :]], axis=0)
    pad_rows = -small.shape[0] % V7X_SUBLANES
    small = jnp.pad(small, ((0, pad_rows), (0, 0)))
    weights = (
        (norm1_g[l][:, None] * w_in[l]).astype(bf16), small,
        _gate_weights(lru_wa[l], lru_wx[l]), w_out[l].astype(bf16),
        (norm2_g[l][:, None] * w_up[l]).astype(bf16), w_down[l].astype(bf16),
        norm_f_g[None, :],
    )

    def rows_first(a):
        return jnp.transpose(a, (1, 0, 2))

    yp, pa, pl_, ph = _run_layer(
        x_prompt, None, weights, seq_per_block=1, chunk=TILE_ROWS,
        conv_widths=conv_widths, reset_first=True, name="layer_prompt")
    sample_states = (rows_first(state_conv_a[l]), rows_first(state_lru_conv[l]),
                     state_lru_h[l])
    ys, sa, sl, sh = _run_layer(
        x_sample, sample_states, weights, seq_per_block=TILE_ROWS // dec_seq,
        chunk=dec_seq, conv_widths=conv_widths, reset_first=False, name="layer_sample")
    return (yp, ys, rows_first(pa)[None], rows_first(pl_)[None], ph[None],
            rows_first(sa)[None], rows_first(sl)[None], sh[None])
```

```python
import functools

import jax
import jax.numpy as jnp
from jax import lax
from jax.experimental import pallas as pl
from jax.experimental.pallas import tpu as pltpu

LRU_C = 8.0
EPS = 1e-6
GELU_K0 = 0.7978845608028654
GELU_K1 = 0.044715

V7X_SUBLANES = 8
V7X_MXU_DIM = 256
V7X_VMEM_LIMIT_BYTES = 58 * 1024 * 1024
HIST_ROWS = V7X_SUBLANES
TILE_ROWS = 512
SCAN_BLOCK = 4


def _rms_scale(x):
    return x * lax.rsqrt(jnp.mean(x * x, axis=-1, keepdims=True) + EPS)


def _gelu_tanh(x):
    return 0.5 * x * (1.0 + jnp.tanh(GELU_K0 * (x + GELU_K1 * (x * x * x))))


def _softplus(x):
    return jnp.maximum(x, 0.0) + jnp.log1p(jnp.exp(-jnp.abs(x)))


def _causal_conv(ext, p_ref, w_row, width, n_seq, seg):
    chans = ext.shape[-1]
    acc = None
    for k in reversed(range(width)):
        lo = HIST_ROWS - (width - 1) + k
        tap = p_ref[w_row + k:w_row + k + 1, :]
        term = ext[:, lo:lo + seg, :].reshape(n_seq * seg, chans) * tap
        acc = term if acc is None else acc + term
    return acc


def _layer_kernel(*refs, n_chunks, conv_widths, reset_first, zero_state, blocked_state):
    refs = list(refs)
    x_ref = refs.pop(0)
    if not zero_state:
        sa_ref, sl_ref, sh_ref = refs[:3]
        del refs[:3]
    (w_in_ref, p_ref, wg_ref, w_out_ref, w_up_ref, w_down_ref, gf_ref,
     y_ref, na_ref, nl_ref, nh_ref,
     ext_a, ext_l, h_c, a_s, b_s, h_s, w_up_v, w_down_v, w_sem) = refs

    def mlp_weight_copies():
        return (pltpu.make_async_copy(w_up_ref, w_up_v, w_sem.at[0]),
                pltpu.make_async_copy(w_down_ref, w_down_v, w_sem.at[1]))
    n_seq, seg, d_model = x_ref.shape
    rows = n_seq * seg
    d_conv = ext_a.shape[-1]
    d_lru = ext_l.shape[-1]
    wa, wl = conv_widths
    caw_row, lcw_row = 0, wa
    lcb_row, ba_row, bx_row, ap_row = (wa + wl + i for i in range(4))
    n_groups = d_lru // V7X_MXU_DIM
    g = pl.program_id(0)
    chunk = g % n_chunks
    seq0 = 0 if blocked_state else (g // n_chunks) * n_seq

    @pl.when(g == 0)
    def _():
        for copy in mlp_weight_copies():
            copy.start()

    @pl.when(chunk == 0)
    def _():
        for k in range(wa - 1):
            row = jnp.zeros((n_seq, d_conv), jnp.float32) if zero_state else sa_ref[k]
            ext_a[:, HIST_ROWS - (wa - 1) + k, :] = row
        for k in range(wl - 1):
            row = jnp.zeros((n_seq, d_lru), jnp.float32) if zero_state else sl_ref[k]
            ext_l[:, HIST_ROWS - (wl - 1) + k, :] = row
        h_c[...] = jnp.zeros(h_c.shape, jnp.float32) if zero_state else sh_ref[...]

    x = x_ref[...].reshape(rows, d_model)
    hn = _rms_scale(x).astype(jnp.bfloat16)
    x_l = jnp.dot(hn, w_in_ref[:, 3 * d_conv:3 * d_conv + d_lru],
                  preferred_element_type=jnp.float32)

    ext_l[:, HIST_ROWS:, :] = x_l.reshape(n_seq, seg, d_lru)
    c = _causal_conv(ext_l, p_ref, lcw_row, wl, n_seq, seg) + p_ref[lcb_row:lcb_row + 1, :]
    c_bf = c.astype(jnp.bfloat16)
    gcxa_proj = jnp.dot(hn, w_in_ref[:, d_conv:3 * d_conv], preferred_element_type=jnp.float32)
    gates = [jnp.dot(c_bf[:, p * V7X_MXU_DIM:(p + 1) * V7X_MXU_DIM], wg_ref[p],
                     preferred_element_type=jnp.float32) for p in range(n_groups)]
    g_l = jnp.dot(hn, w_in_ref[:, 3 * d_conv + d_lru:], preferred_element_type=jnp.float32)
    g_b = jnp.dot(hn, w_in_ref[:, :d_conv], preferred_element_type=jnp.float32)
    half_rate = (-0.5 * LRU_C) * _softplus(p_ref[ap_row:ap_row + 1, :])
    half_ba = 0.5 * p_ref[ba_row:ba_row + 1, :]
    half_bx = 0.5 * p_ref[bx_row:bx_row + 1, :]
    for p in range(n_groups):
        cols = slice(p * V7X_MXU_DIM, (p + 1) * V7X_MXU_DIM)
        t_r = jnp.tanh(gates[p][:, :V7X_MXU_DIM] + half_ba[:, cols])
        t_i = jnp.tanh(gates[p][:, V7X_MXU_DIM:] + half_bx[:, cols])
        i_gate = 0.5 * t_i + 0.5
        a = jnp.exp(t_r * half_rate[:, cols] + half_rate[:, cols])
        one_m_a2 = 1.0 - a * a
        mult = jnp.where(one_m_a2 > 0.0, one_m_a2 * lax.rsqrt(one_m_a2), 0.0)
        ic = i_gate * c[:, cols]
        a_s[:, cols] = a
        b_s[:, cols] = mult * ic
        if reset_first:
            for s in range(n_seq):
                r0 = s * seg
                b_s[r0:r0 + 1, cols] = jnp.where(
                    chunk == 0, ic[r0:r0 + 1, :], (mult * ic)[r0:r0 + 1, :])

    assert seg % SCAN_BLOCK == 0
    h_in = None
    for r in range(0, rows, SCAN_BLOCK):
        s, t = divmod(r, seg)
        if t == 0:
            h_in = h_c[s:s + 1, :]
        a0, a1, a2, a3 = (a_s[r + k:r + k + 1, :] for k in range(SCAN_BLOCK))
        b0, b1, b2, b3 = (b_s[r + k:r + k + 1, :] for k in range(SCAN_BLOCK))
        a01 = a1 * a0
        b01 = a1 * b0 + b1
        a23 = a3 * a2
        b23 = a3 * b2 + b3
        h0 = a0 * h_in + b0
        h1 = a01 * h_in + b01
        h2 = a2 * h1 + b2
        h3 = (a23 * a01) * h_in + (a23 * b01 + b23)
        for k, h in enumerate((h0, h1, h2, h3)):
            h_s[r + k:r + k + 1, :] = h
        h_in = h3
        if t + SCAN_BLOCK == seg:
            h_c[s:s + 1, :] = h3

    g_c = gcxa_proj[:, :d_conv]
    x_a = gcxa_proj[:, d_conv:]
    ext_a[:, HIST_ROWS:, :] = (g_c * x_a).reshape(n_seq, seg, d_conv)
    out_a = g_b * _causal_conv(ext_a, p_ref, caw_row, wa, n_seq, seg)

    for k in range(wa - 1):
        na_ref[k, pl.ds(seq0, n_seq), :] = ext_a[:, seg + HIST_ROWS - (wa - 1) + k, :]
    ext_a[:, HIST_ROWS - (wa - 1):HIST_ROWS, :] = (
        ext_a[:, seg + HIST_ROWS - (wa - 1):seg + HIST_ROWS, :])
    for k in range(wl - 1):
        nl_ref[k, pl.ds(seq0, n_seq), :] = ext_l[:, seg + HIST_ROWS - (wl - 1) + k, :]
    ext_l[:, HIST_ROWS - (wl - 1):HIST_ROWS, :] = (
        ext_l[:, seg + HIST_ROWS - (wl - 1):seg + HIST_ROWS, :])
    nh_ref[pl.ds(seq0, n_seq), :] = h_c[...]

    out_b = h_s[...] * _gelu_tanh(g_l)
    mix = jnp.concatenate([out_a, out_b], axis=-1).astype(jnp.bfloat16)
    x1 = x + jnp.dot(mix, w_out_ref[...], preferred_element_type=jnp.float32)
    hm = _rms_scale(x1).astype(jnp.bfloat16)

    @pl.when(g == 0)
    def _():
        for copy in mlp_weight_copies():
            copy.wait()

    up = jnp.dot(hm, w_up_v[...], preferred_element_type=jnp.float32)
    act = jnp.square(jnp.maximum(up, 0.0)).astype(jnp.bfloat16)
    x2 = x1 + jnp.dot(act, w_down_v[...], preferred_element_type=jnp.float32)
    y_ref[...] = (_rms_scale(x2) * gf_ref[...]).reshape(n_seq, seg, d_model)


def _resident(shape):
    zeros = (0,) * len(shape)
    return pl.BlockSpec(shape, lambda g: zeros, pipeline_mode=pl.Buffered(1))


def _run_layer(x, states, weights, *, seq_per_block, chunk, conv_widths, reset_first, name):
    n_seq, t_len, d_model = x.shape
    wa, wl = conv_widths
    chans = weights[1].shape[-1]
    assert n_seq % seq_per_block == 0 and t_len % chunk == 0
    assert chunk % V7X_SUBLANES == 0
    rows = seq_per_block * chunk
    n_chunks = t_len // chunk
    n_tiles = (n_seq // seq_per_block) * n_chunks
    blocked_state = seq_per_block % V7X_SUBLANES == 0
    zero_state = states is None
    assert zero_state or blocked_state

    def rows_spec(n_rows):
        if blocked_state:
            return pl.BlockSpec((n_rows, seq_per_block, chans), lambda g: (0, g // n_chunks, 0))
        return pl.BlockSpec((n_rows, n_seq, chans), lambda g: (0, 0, 0))

    if blocked_state:
        h_spec = pl.BlockSpec((seq_per_block, chans), lambda g: (g // n_chunks, 0))
    else:
        h_spec = pl.BlockSpec((n_seq, chans), lambda g: (0, 0))
    x_spec = pl.BlockSpec((seq_per_block, chunk, d_model),
                          lambda g: (g // n_chunks, g % n_chunks, 0))

    operands = [x]
    in_specs = [x_spec]
    if not zero_state:
        operands += list(states)
        in_specs += [rows_spec(wa - 1), rows_spec(wl - 1), h_spec]
    operands += list(weights)
    w_up, w_down = weights[4], weights[5]
    in_specs += [pl.BlockSpec(memory_space=pl.ANY) if w is w_up or w is w_down
                 else _resident(w.shape) for w in weights]
    f32 = jnp.float32
    out_shape = (
        jax.ShapeDtypeStruct(x.shape, f32),
        jax.ShapeDtypeStruct((wa - 1, n_seq, chans), f32),
        jax.ShapeDtypeStruct((wl - 1, n_seq, chans), f32),
        jax.ShapeDtypeStruct((n_seq, chans), f32),
    )
    out_specs = (x_spec, rows_spec(wa - 1), rows_spec(wl - 1), h_spec)
    scratch = [
        pltpu.VMEM((seq_per_block, HIST_ROWS + chunk, chans), f32),
        pltpu.VMEM((seq_per_block, HIST_ROWS + chunk, chans), f32),
        pltpu.VMEM((seq_per_block, chans), f32),
        pltpu.VMEM((rows, chans), f32),
        pltpu.VMEM((rows, chans), f32),
        pltpu.VMEM((rows, chans), f32),
        pltpu.VMEM(w_up.shape, w_up.dtype),
        pltpu.VMEM(w_down.shape, w_down.dtype),
        pltpu.SemaphoreType.DMA((2,)),
    ]
    return pl.pallas_call(
        functools.partial(
            _layer_kernel, n_chunks=n_chunks, conv_widths=conv_widths,
            reset_first=reset_first, zero_state=zero_state, blocked_state=blocked_state),
        grid=(n_tiles,),
        in_specs=in_specs,
        out_specs=out_specs,
        out_shape=out_shape,
        scratch_shapes=scratch,
        compiler_params=pltpu.CompilerParams(
            dimension_semantics=("arbitrary",),
            vmem_limit_bytes=V7X_VMEM_LIMIT_BYTES),
        name=name,
    )(*operands)


def _block_diag(w):
    n_heads, d, _ = w.shape
    eye = jnp.eye(n_heads, dtype=w.dtype)
    return (w[:, :, None, :] * eye[:, None, :, None]).reshape(n_heads * d, n_heads * d)


def _gate_weights(wa, wx):
    bd_a, bd_x = _block_diag(wa), _block_diag(wx)
    d_lru = bd_a.shape[0]
    groups = []
    for p in range(d_lru // V7X_MXU_DIM):
        sl = slice(p * V7X_MXU_DIM, (p + 1) * V7X_MXU_DIM)
        groups.append(jnp.concatenate([bd_a[sl, sl], bd_x[sl, sl]], axis=1))
    return (0.5 * jnp.stack(groups)).astype(jnp.bfloat16)


def kernel(x_prompt, x_sample, state_conv_a, state_lru_conv, state_lru_h, norm1_g, w_in, conv_a_w, lru_conv_w, lru_conv_b, lru_wa, lru_ba, lru_wx, lru_bx, lru_a_param, w_out, norm2_g, w_up, w_down, norm_f_g):
    depth = w_in.shape[0]
    assert depth == 1
    dec_seq = x_sample.shape[1]
    d_conv = conv_a_w.shape[-1]
    d_lru = lru_conv_w.shape[-1]
    assert d_conv == d_lru
    conv_widths = (conv_a_w.shape[1], lru_conv_w.shape[1])
    bf16 = jnp.bfloat16
    l = 0
    small = jnp.concatenate([
        conv_a_w[l], lru_conv_w[l], lru_conv_b[l][None, :], lru_ba[l].reshape(1, d_lru),
        lru_bx[l].reshape(1, d_lru), lru_a_param[l][None, :]], axis=0)
    pad_rows = -small.shape[0] % V7X_SUBLANES
    small = jnp.pad(small, ((0, pad_rows), (0, 0)))
    weights = (
        (norm1_g[l][:, None] * w_in[l]).astype(bf16), small,
        _gate_weights(lru_wa[l], lru_wx[l]), w_out[l].astype(bf16),
        (norm2_g[l][:, None] * w_up[l]).astype(bf16), w_down[l].astype(bf16),
        norm_f_g[None, :],
    )

    def rows_first(a):
        return jnp.transpose(a, (1, 0, 2))

    yp, pa, pl_, ph = _run_layer(
        x_prompt, None, weights, seq_per_block=1, chunk=TILE_ROWS,
        conv_widths=conv_widths, reset_first=True, name="layer_prompt")
    sample_states = (rows_first(state_conv_a[l]), rows_first(state_lru_conv[l]),
                     state_lru_h[l])
    ys, sa, sl, sh = _run_layer(
        x_sample, sample_states, weights, seq_per_block=TILE_ROWS // dec_seq,
        chunk=dec_seq, conv_widths=conv_widths, reset_first=False, name="layer_sample")
    return (yp, ys, rows_first(pa)[None], rows_first(pl_)[None], ph[None],
            rows_first(sa)[None], rows_first(sl)[None], sh[None])
```

```python
import functools

import jax
import jax.numpy as jnp
from jax import lax
from jax.experimental import pallas as pl
from jax.experimental.pallas import tpu as pltpu

LRU_C = 8.0
EPS = 1e-6
GELU_K0 = 0.7978845608028654
GELU_K1 = 0.044715

V7X_SUBLANES = 8
V7X_MXU_DIM = 256
V7X_VMEM_LIMIT_BYTES = 58 * 1024 * 1024
HIST_ROWS = V7X_SUBLANES
TILE_ROWS = 512
SCAN_BLOCK = 4


def _rms_scale(x):
    return x * lax.rsqrt(jnp.mean(x * x, axis=-1, keepdims=True) + EPS)


def _gelu_tanh(x):
    return 0.5 * x * (1.0 + jnp.tanh(GELU_K0 * (x + GELU_K1 * (x * x * x))))


def _softplus(x):
    return jnp.maximum(x, 0.0) + jnp.log1p(jnp.exp(-jnp.abs(x)))


def _causal_conv(ext, p_ref, w_row, width, n_seq, seg):
    chans = ext.shape[-1]
    acc = None
    for k in reversed(range(width)):
        lo = HIST_ROWS - (width - 1) + k
        tap = p_ref[w_row + k:w_row + k + 1, :]
        term = ext[:, lo:lo + seg, :].reshape(n_seq * seg, chans) * tap
        acc = term if acc is None else acc + term
    return acc


def _layer_kernel(*refs, n_chunks, conv_widths, reset_first, zero_state, blocked_state):
    refs = list(refs)
    x_ref = refs.pop(0)
    if not zero_state:
        sa_ref, sl_ref, sh_ref = refs[:3]
        del refs[:3]
    (w_in_ref, p_ref, wg_ref, w_out_ref, w_up_ref, w_down_ref, gf_ref,
     y_ref, na_ref, nl_ref, nh_ref,
     ext_a, ext_l, h_c, a_s, b_s, h_s, w_up_v, w_down_v, wg_v, w_out_v, w_sem) = refs

    def mlp_weight_copies():
        return (pltpu.make_async_copy(w_up_ref, w_up_v, w_sem.at[0]),
                pltpu.make_async_copy(w_down_ref, w_down_v, w_sem.at[1]))

    def mixer_weight_copies():
        return (pltpu.make_async_copy(wg_ref, wg_v, w_sem.at[2]),
                pltpu.make_async_copy(w_out_ref, w_out_v, w_sem.at[3]))
    n_seq, seg, d_model = x_ref.shape
    rows = n_seq * seg
    d_conv = ext_a.shape[-1]
    d_lru = ext_l.shape[-1]
    wa, wl = conv_widths
    caw_row, lcw_row = 0, wa
    lcb_row, ba_row, bx_row, ap_row = (wa + wl + i for i in range(4))
    n_groups = d_lru // V7X_MXU_DIM
    g = pl.program_id(0)
    chunk = g % n_chunks
    seq0 = 0 if blocked_state else (g // n_chunks) * n_seq

    @pl.when(g == 0)
    def _():
        for copy in mixer_weight_copies() + mlp_weight_copies():
            copy.start()

    @pl.when(chunk == 0)
    def _():
        for k in range(wa - 1):
            row = jnp.zeros((n_seq, d_conv), jnp.float32) if zero_state else sa_ref[k]
            ext_a[:, HIST_ROWS - (wa - 1) + k, :] = row
        for k in range(wl - 1):
            row = jnp.zeros((n_seq, d_lru), jnp.float32) if zero_state else sl_ref[k]
            ext_l[:, HIST_ROWS - (wl - 1) + k, :] = row
        h_c[...] = jnp.zeros(h_c.shape, jnp.float32) if zero_state else sh_ref[...]

    x = x_ref[...].reshape(rows, d_model)
    hn = _rms_scale(x).astype(jnp.bfloat16)
    x_l = jnp.dot(hn, w_in_ref[:, 3 * d_conv:3 * d_conv + d_lru],
                  preferred_element_type=jnp.float32)

    ext_l[:, HIST_ROWS:, :] = x_l.reshape(n_seq, seg, d_lru)
    c = _causal_conv(ext_l, p_ref, lcw_row, wl, n_seq, seg) + p_ref[lcb_row:lcb_row + 1, :]
    c_bf = c.astype(jnp.bfloat16)
    gcxa_proj = jnp.dot(hn, w_in_ref[:, d_conv:3 * d_conv], preferred_element_type=jnp.float32)
    @pl.when(g == 0)
    def _():
        for copy in mixer_weight_copies():
            copy.wait()

    gates = [jnp.dot(c_bf[:, p * V7X_MXU_DIM:(p + 1) * V7X_MXU_DIM], wg_v[p],
                     preferred_element_type=jnp.float32) for p in range(n_groups)]
    g_l = jnp.dot(hn, w_in_ref[:, 3 * d_conv + d_lru:], preferred_element_type=jnp.float32)
    g_b = jnp.dot(hn, w_in_ref[:, :d_conv], preferred_element_type=jnp.float32)
    half_rate = (-0.5 * LRU_C) * _softplus(p_ref[ap_row:ap_row + 1, :])
    half_ba = 0.5 * p_ref[ba_row:ba_row + 1, :]
    half_bx = 0.5 * p_ref[bx_row:bx_row + 1, :]
    for p in range(n_groups):
        cols = slice(p * V7X_MXU_DIM, (p + 1) * V7X_MXU_DIM)
        t_r = jnp.tanh(gates[p][:, :V7X_MXU_DIM] + half_ba[:, cols])
        t_i = jnp.tanh(gates[p][:, V7X_MXU_DIM:] + half_bx[:, cols])
        i_gate = 0.5 * t_i + 0.5
        a = jnp.exp(t_r * half_rate[:, cols] + half_rate[:, cols])
        one_m_a2 = 1.0 - a * a
        mult = jnp.where(one_m_a2 > 0.0, one_m_a2 * lax.rsqrt(one_m_a2), 0.0)
        ic = i_gate * c[:, cols]
        a_s[:, cols] = a
        b_s[:, cols] = mult * ic
        if reset_first:
            for s in range(n_seq):
                r0 = s * seg
                b_s[r0:r0 + 1, cols] = jnp.where(
                    chunk == 0, ic[r0:r0 + 1, :], (mult * ic)[r0:r0 + 1, :])

    assert seg % SCAN_BLOCK == 0
    h_in = None
    for r in range(0, rows, SCAN_BLOCK):
        s, t = divmod(r, seg)
        if t == 0:
            h_in = h_c[s:s + 1, :]
        a0, a1, a2, a3 = (a_s[r + k:r + k + 1, :] for k in range(SCAN_BLOCK))
        b0, b1, b2, b3 = (b_s[r + k:r + k + 1, :] for k in range(SCAN_BLOCK))
        a01 = a1 * a0
        b01 = a1 * b0 + b1
        a23 = a3 * a2
        b23 = a3 * b2 + b3
        h0 = a0 * h_in + b0
        h1 = a01 * h_in + b01
        h2 = a2 * h1 + b2
        h3 = (a23 * a01) * h_in + (a23 * b01 + b23)
        for k, h in enumerate((h0, h1, h2, h3)):
            h_s[r + k:r + k + 1, :] = h
        h_in = h3
        if t + SCAN_BLOCK == seg:
            h_c[s:s + 1, :] = h3

    g_c = gcxa_proj[:, :d_conv]
    x_a = gcxa_proj[:, d_conv:]
    ext_a[:, HIST_ROWS:, :] = (g_c * x_a).reshape(n_seq, seg, d_conv)
    out_a = g_b * _causal_conv(ext_a, p_ref, caw_row, wa, n_seq, seg)

    for k in range(wa - 1):
        na_ref[k, pl.ds(seq0, n_seq), :] = ext_a[:, seg + HIST_ROWS - (wa - 1) + k, :]
    ext_a[:, HIST_ROWS - (wa - 1):HIST_ROWS, :] = (
        ext_a[:, seg + HIST_ROWS - (wa - 1):seg + HIST_ROWS, :])
    for k in range(wl - 1):
        nl_ref[k, pl.ds(seq0, n_seq), :] = ext_l[:, seg + HIST_ROWS - (wl - 1) + k, :]
    ext_l[:, HIST_ROWS - (wl - 1):HIST_ROWS, :] = (
        ext_l[:, seg + HIST_ROWS - (wl - 1):seg + HIST_ROWS, :])
    nh_ref[pl.ds(seq0, n_seq), :] = h_c[...]

    out_b = h_s[...] * _gelu_tanh(g_l)
    mix = jnp.concatenate([out_a, out_b], axis=-1).astype(jnp.bfloat16)
    x1 = x + jnp.dot(mix, w_out_v[...], preferred_element_type=jnp.float32)
    hm = _rms_scale(x1).astype(jnp.bfloat16)

    @pl.when(g == 0)
    def _():
        for copy in mlp_weight_copies():
            copy.wait()

    up = jnp.dot(hm, w_up_v[...], preferred_element_type=jnp.float32)
    act = jnp.square(jnp.maximum(up, 0.0)).astype(jnp.bfloat16)
    x2 = x1 + jnp.dot(act, w_down_v[...], preferred_element_type=jnp.float32)
    y_ref[...] = (_rms_scale(x2) * gf_ref[...]).reshape(n_seq, seg, d_model)


def _resident(shape):
    zeros = (0,) * len(shape)
    return pl.BlockSpec(shape, lambda g: zeros, pipeline_mode=pl.Buffered(1))


def _run_layer(x, states, weights, *, seq_per_block, chunk, conv_widths, reset_first, name):
    n_seq, t_len, d_model = x.shape
    wa, wl = conv_widths
    chans = weights[1].shape[-1]
    assert n_seq % seq_per_block == 0 and t_len % chunk == 0
    assert chunk % V7X_SUBLANES == 0
    rows = seq_per_block * chunk
    n_chunks = t_len // chunk
    n_tiles = (n_seq // seq_per_block) * n_chunks
    blocked_state = seq_per_block % V7X_SUBLANES == 0
    zero_state = states is None
    assert zero_state or blocked_state

    def rows_spec(n_rows):
        if blocked_state:
            return pl.BlockSpec((n_rows, seq_per_block, chans), lambda g: (0, g // n_chunks, 0))
        return pl.BlockSpec((n_rows, n_seq, chans), lambda g: (0, 0, 0))

    if blocked_state:
        h_spec = pl.BlockSpec((seq_per_block, chans), lambda g: (g // n_chunks, 0))
    else:
        h_spec = pl.BlockSpec((n_seq, chans), lambda g: (0, 0))
    x_spec = pl.BlockSpec((seq_per_block, chunk, d_model),
                          lambda g: (g // n_chunks, g % n_chunks, 0))

    operands = [x]
    in_specs = [x_spec]
    if not zero_state:
        operands += list(states)
        in_specs += [rows_spec(wa - 1), rows_spec(wl - 1), h_spec]
    operands += list(weights)
    wg, w_out, w_up, w_down = weights[2:6]
    in_specs += [pl.BlockSpec(memory_space=pl.ANY) if 2 <= i < 6 else _resident(w.shape)
                 for i, w in enumerate(weights)]
    f32 = jnp.float32
    out_shape = (
        jax.ShapeDtypeStruct(x.shape, f32),
        jax.ShapeDtypeStruct((wa - 1, n_seq, chans), f32),
        jax.ShapeDtypeStruct((wl - 1, n_seq, chans), f32),
        jax.ShapeDtypeStruct((n_seq, chans), f32),
    )
    out_specs = (x_spec, rows_spec(wa - 1), rows_spec(wl - 1), h_spec)
    scratch = [
        pltpu.VMEM((seq_per_block, HIST_ROWS + chunk, chans), f32),
        pltpu.VMEM((seq_per_block, HIST_ROWS + chunk, chans), f32),
        pltpu.VMEM((seq_per_block, chans), f32),
        pltpu.VMEM((rows, chans), f32),
        pltpu.VMEM((rows, chans), f32),
        pltpu.VMEM((rows, chans), f32),
        pltpu.VMEM(w_up.shape, w_up.dtype),
        pltpu.VMEM(w_down.shape, w_down.dtype),
        pltpu.VMEM(wg.shape, wg.dtype),
        pltpu.VMEM(w_out.shape, w_out.dtype),
        pltpu.SemaphoreType.DMA((4,)),
    ]
    return pl.pallas_call(
        functools.partial(
            _layer_kernel, n_chunks=n_chunks, conv_widths=conv_widths,
            reset_first=reset_first, zero_state=zero_state, blocked_state=blocked_state),
        grid=(n_tiles,),
        in_specs=in_specs,
        out_specs=out_specs,
        out_shape=out_shape,
        scratch_shapes=scratch,
        compiler_params=pltpu.CompilerParams(
            dimension_semantics=("arbitrary",),
            vmem_limit_bytes=V7X_VMEM_LIMIT_BYTES),
        name=name,
    )(*operands)


def _block_diag(w):
    n_heads, d, _ = w.shape
    eye = jnp.eye(n_heads, dtype=w.dtype)
    return (w[:, :, None, :] * eye[:, None, :, None]).reshape(n_heads * d, n_heads * d)


def _gate_weights(wa, wx):
    bd_a, bd_x = _block_diag(wa), _block_diag(wx)
    d_lru = bd_a.shape[0]
    groups = []
    for p in range(d_lru // V7X_MXU_DIM):
        sl = slice(p * V7X_MXU_DIM, (p + 1) * V7X_MXU_DIM)
        groups.append(jnp.concatenate([bd_a[sl, sl], bd_x[sl, sl]], axis=1))
    return (0.5 * jnp.stack(groups)).astype(jnp.bfloat16)


def kernel(x_prompt, x_sample, state_conv_a, state_lru_conv, state_lru_h, norm1_g, w_in, conv_a_w, lru_conv_w, lru_conv_b, lru_wa, lru_ba, lru_wx, lru_bx, lru_a_param, w_out, norm2_g, w_up, w_down, norm_f_g):
    depth = w_in.shape[0]
    assert depth == 1
    dec_seq = x_sample.shape[1]
    d_conv = conv_a_w.shape[-1]
    d_lru = lru_conv_w.shape[-1]
    assert d_conv == d_lru
    conv_widths = (conv_a_w.shape[1], lru_conv_w.shape[1])
    bf16 = jnp.bfloat16
    l = 0
    small = jnp.concatenate([
        conv_a_w[l], lru_conv_w[l], lru_conv_b[l][None, :], lru_ba[l].reshape(1, d_lru),
        lru_bx[l].reshape(1, d_lru), lru_a_param[l][None, :]], axis=0)
    pad_rows = -small.shape[0] % V7X_SUBLANES
    small = jnp.pad(small, ((0, pad_rows), (0, 0)))
    weights = (
        (norm1_g[l][:, None] * w_in[l]).astype(bf16), small,
        _gate_weights(lru_wa[l], lru_wx[l]), w_out[l].astype(bf16),
        (norm2_g[l][:, None] * w_up[l]).astype(bf16), w_down[l].astype(bf16),
        norm_f_g[None, :],
    )

    def rows_first(a):
        return jnp.transpose(a, (1, 0, 2))

    yp, pa, pl_, ph = _run_layer(
        x_prompt, None, weights, seq_per_block=1, chunk=TILE_ROWS,
        conv_widths=conv_widths, reset_first=True, name="layer_prompt")
    sample_states = (rows_first(state_conv_a[l]), rows_first(state_lru_conv[l]),
                     state_lru_h[l])
    ys, sa, sl, sh = _run_layer(
        x_sample, sample_states, weights, seq_per_block=TILE_ROWS // dec_seq,
        chunk=dec_seq, conv_widths=conv_widths, reset_first=False, name="layer_sample")
    return (yp, ys, rows_first(pa)[None], rows_first(pl_)[None], ph[None],
            rows_first(sa)[None], rows_first(sl)[None], sh[None])
```
